```python
import math, functools
import jax, jax.numpy as jnp
from jax import lax
import numpy as np

D_MODEL = 1024
BATCH = 8
SEQ = 2048
DEPTH = 2
DEC_BATCH = 32
DEC_SEQ = 8
PAST_LEN = 16384
PAGE_SIZE = 128

N_META = 16
N_HEADS = D_MODEL // 128
HEAD_DIM = 64
ATT_W = N_HEADS * HEAD_DIM
IDX_HEADS = 8
IDX_DIM = 64
TOPK_MAX = 256
QBLOCK = 128
HG_HEADS = D_MODEL // 256
HG_DK = 128
HG_DV = 128
HG_W = HG_HEADS * HG_DK
HG_CHUNK = 64
D_FF = ((8 * D_MODEL // 3) + 127) // 128 * 128
IN_SIZES = (ATT_W, ATT_W, ATT_W, IDX_HEADS * IDX_DIM, IDX_HEADS, IDX_DIM,
            HG_W, HG_W, HG_HEADS * HG_DV, HG_HEADS * HG_DV, D_MODEL, D_MODEL)
N_IN = sum(IN_SIZES)
RMS_EPS = 1e-6
NEG_INF = -1e30
LB_FLOOR = 1e-30

kernel_name = "hybrid_dsa_hgrn2_macaron_step"


def rms_norm(x, g):
    xf = x.astype(jnp.float32)
    y = xf * lax.rsqrt(jnp.mean(xf * xf, axis=-1, keepdims=True) + RMS_EPS)
    return (y * g.astype(jnp.float32)).astype(x.dtype)


def swiglu(h, w_gu, w_dn):
    g, u = jnp.split(h @ w_gu, 2, axis=-1)
    return (jax.nn.silu(g) * u) @ w_dn


def split_columns(z):
    out, start = [], 0
    for size in IN_SIZES:
        out.append(z[..., start:start + size])
        start += size
    return out


def indexer_scores(qi, wi, ki):
    dots = jnp.einsum('...qhd,...ld->...qhl', qi, ki).astype(jnp.float32) * (IDX_DIM ** -0.5)
    return jnp.einsum('...qh,...qhl->...ql', wi.astype(jnp.float32) * (IDX_HEADS ** -0.5), jax.nn.relu(dots))


def select_keys(scores, qpos, kpos, topk):
    adm = kpos[None, :] <= qpos[:, None]
    _, idx = lax.top_k(jnp.where(adm, scores, NEG_INF), topk)
    valid = idx <= qpos[:, None]
    return idx, valid


def sparse_attend(q, kg, vg, valid):
    s = jnp.einsum('...qhd,...qkhd->...qhk', q, kg).astype(jnp.float32) * (HEAD_DIM ** -0.5)
    s = jnp.where(valid[..., :, None, :], s, NEG_INF)
    p = jax.nn.softmax(s, axis=-1).astype(vg.dtype)
    return jnp.einsum('...qhk,...qkhd->...qhd', p, vg)


def dsa_prompt(q, k, v, qi, wi, ki, topk):
    B, T = q.shape[:2]
    nqb = -(-T // QBLOCK)
    tp = nqb * QBLOCK

    def blocks(a):
        a = jnp.pad(a, [(0, 0), (0, tp - T)] + [(0, 0)] * (a.ndim - 2))
        return a.reshape((B, nqb, QBLOCK) + a.shape[2:])

    qpos = jnp.arange(tp, dtype=jnp.int32).reshape(nqb, QBLOCK)
    kpos = jnp.arange(T, dtype=jnp.int32)

    def per_seq(seq):
        q_s, qi_s, wi_s, k_s, v_s, ki_s = seq

        def per_block(blk):
            qq, qqi, ww, pos = blk
            idx, valid = select_keys(indexer_scores(qqi, ww, ki_s), pos, kpos, topk)
            return sparse_attend(qq, k_s[idx], v_s[idx], valid)

        return lax.map(per_block, (q_s, qi_s, wi_s, qpos))

    out = lax.map(per_seq, (blocks(q), blocks(qi), blocks(wi), k, v, ki))
    return out.reshape(B, tp, N_HEADS, HEAD_DIM)[:, :T]


def dsa_sample(q, k, v, qi, wi, ki, cache_k, cache_v, cache_kidx, page_table, layer, topk):
    DB, DS = q.shape[:2]
    past = page_table.shape[1] * PAGE_SIZE
    ki_past = cache_kidx[layer, page_table].reshape(DB, past, IDX_DIM).astype(ki.dtype)
    ki_all = jnp.concatenate([ki_past, ki], axis=1)
    qpos = past + jnp.arange(DS, dtype=jnp.int32)
    kpos = jnp.arange(past + DS, dtype=jnp.int32)
    idx, valid = select_keys(indexer_scores(qi, wi, ki_all), qpos, kpos, topk)
    in_past = (idx < past)[..., None, None]
    pidx = jnp.minimum(idx, past - 1)
    bidx = jnp.arange(DB)[:, None, None]
    phys = page_table[bidx, pidx // PAGE_SIZE]
    off = pidx % PAGE_SIZE
    nidx = jnp.clip(idx - past, 0, DS - 1)
    kg = jnp.where(in_past, cache_k[layer, phys, off].astype(k.dtype), k[bidx, nidx])
    vg = jnp.where(in_past, cache_v[layer, phys, off].astype(v.dtype), v[bidx, nidx])
    return sparse_attend(q, kg, vg, valid)


def forget_log(z, lb):
    lbf = jnp.maximum(lb.astype(jnp.float32), LB_FLOOR)
    return jnp.logaddexp(jnp.log(lbf), jnp.log1p(-lbf) + jax.nn.log_sigmoid(z.astype(jnp.float32)))


def hgrn2_chunk_scan(q, k, v, logf, S0, chunk):
    B, L, H, _ = q.shape
    n = L // chunk

    def to_chunks(a):
        a = a.astype(jnp.float32)
        return a.reshape(B, n, chunk, H, a.shape[-1]).transpose(1, 0, 3, 2, 4)

    causal = jnp.tril(jnp.ones((chunk, chunk), bool))[:, :, None]

    def step(S, inp):
        qc, kc, vc, gc = inp
        b = jnp.cumsum(gc, axis=2)
        diff = b[:, :, :, None, :] - b[:, :, None, :, :]
        decay = jnp.where(causal, jnp.exp(jnp.minimum(diff, 0.0)), 0.0)
        A = jnp.einsum('bhtd,bhsd,bhtsd->bhts', qc, kc, decay)
        o = jnp.einsum('bhtd,bhde->bhte', qc * jnp.exp(b), S) + jnp.einsum('bhts,bhse->bhte', A, vc)
        bC = b[:, :, -1:, :]
        S_new = jnp.exp(bC[:, :, 0, :])[..., None] * S + jnp.einsum('bhsd,bhse->bhde', kc * jnp.exp(bC - b), vc)
        return S_new, o

    S, os = lax.scan(step, S0.astype(jnp.float32), (to_chunks(q), to_chunks(k), to_chunks(v), to_chunks(logf)))
    o = os.transpose(1, 0, 3, 2, 4).reshape(B, L, H, v.shape[-1])
    return o, S


def hgrn_prompt(q, k, v, logf):
    S0 = jnp.zeros((q.shape[0], HG_HEADS, HG_DK, HG_DV), jnp.float32)
    m = N_META
    o_meta, S = hgrn2_chunk_scan(q[:, :m], k[:, :m], v[:, :m], logf[:, :m], S0, m)
    o_real, S = hgrn2_chunk_scan(q[:, m:], k[:, m:], v[:, m:], logf[:, m:], S, HG_CHUNK)
    return jnp.concatenate([o_meta, o_real], axis=1), S


def layer_forward(x, gains, f1_gu, f1_dn, w_in, lb, hg_g, w_pa, w_ph, w_o, f2_gu, f2_dn, attn_fn, hgrn_fn):
    B, L, _ = x.shape
    x = x + 0.5 * rms_norm(swiglu(rms_norm(x, gains[0]), f1_gu, f1_dn), gains[1])
    h = rms_norm(x, gains[2])
    q, k, v, qi, wi, ki, hq, hf, hv, hgate, ga, gh = split_columns(h @ w_in)

    def heads(a, n):
        return a.reshape(B, L, n, -1)

    q, k, v = heads(q, N_HEADS), heads(k, N_HEADS), heads(v, N_HEADS)
    ya = attn_fn(q, k, v, heads(qi, IDX_HEADS), wi, ki).reshape(B, L, ATT_W)
    logf = forget_log(heads(hf, HG_HEADS), lb.reshape(HG_HEADS, HG_DK))
    o, S = hgrn_fn(jax.nn.silu(heads(hq, HG_HEADS)), -jnp.expm1(logf), heads(hv, HG_HEADS), logf)
    yh = (rms_norm(o.astype(x.dtype), hg_g) * jax.nn.silu(heads(hgate, HG_HEADS))).reshape(B, L, HG_W)
    m = jax.nn.sigmoid(ga) * (ya @ w_pa) + jax.nn.sigmoid(gh) * (yh @ w_ph)
    x = x + rms_norm(m @ w_o, gains[3])
    x = x + 0.5 * rms_norm(swiglu(rms_norm(x, gains[4]), f2_gu, f2_dn), gains[5])
    return x, (k, v, ki, S.astype(x.dtype))


def setup_inputs(seed: int = 0) -> dict:
    key = jax.random.key(seed)
    ks = jax.random.split(key, 20)
    f32 = jnp.float32
    n_pages = PAST_LEN // PAGE_SIZE
    n_used = DEC_BATCH * n_pages
    n_pool = n_used + (n_used + 3) // 4
    nrm = lambda k, shape, s=1.0: jax.random.normal(k, shape, f32) * s
    page_table = jax.random.permutation(ks[0], n_pool)[:n_used].reshape(DEC_BATCH, n_pages).astype(jnp.int32)
    return {
        "x_prompt": nrm(ks[1], (BATCH, SEQ, D_MODEL)),
        "x_sample": nrm(ks[2], (DEC_BATCH, DEC_SEQ, D_MODEL)),
        "cache_k": nrm(ks[3], (DEPTH, n_pool, PAGE_SIZE, N_HEADS, HEAD_DIM)),
        "cache_v": nrm(ks[4], (DEPTH, n_pool, PAGE_SIZE, N_HEADS, HEAD_DIM)),
        "cache_kidx": nrm(ks[5], (DEPTH, n_pool, PAGE_SIZE, IDX_DIM)),
        "state_hgrn": nrm(ks[6], (DEPTH, DEC_BATCH, HG_HEADS, HG_DK, HG_DV), 0.5),
        "page_table": page_table,
        "meta_tokens": nrm(ks[7], (N_META, D_MODEL)),
        "norm_gains": 1.0 + nrm(ks[8], (DEPTH, 6, D_MODEL), 0.02),
        "ffn1_w_gu": nrm(ks[9], (DEPTH, D_MODEL, 2 * D_FF), D_MODEL ** -0.5),
        "ffn1_w_down": nrm(ks[10], (DEPTH, D_FF, D_MODEL), D_FF ** -0.5),
        "w_in": nrm(ks[11], (DEPTH, D_MODEL, N_IN), D_MODEL ** -0.5),
        "lb_raw": 1.0 + nrm(ks[12], (DEPTH, HG_W), 0.1),
        "hg_norm": 1.0 + nrm(ks[13], (DEPTH, HG_DV), 0.02),
        "w_branch_attn": nrm(ks[14], (DEPTH, ATT_W, D_MODEL), ATT_W ** -0.5),
        "w_branch_hgrn": nrm(ks[15], (DEPTH, HG_W, D_MODEL), HG_W ** -0.5),
        "w_out": nrm(ks[16], (DEPTH, D_MODEL, D_MODEL), D_MODEL ** -0.5),
        "ffn2_w_gu": nrm(ks[17], (DEPTH, D_MODEL, 2 * D_FF), D_MODEL ** -0.5),
        "ffn2_w_down": nrm(ks[18], (DEPTH, D_FF, D_MODEL), D_FF ** -0.5),
    }


def reference(x_prompt, x_sample, cache_k, cache_v, cache_kidx, state_hgrn, page_table, meta_tokens,
              norm_gains, ffn1_w_gu, ffn1_w_down, w_in, lb_raw, hg_norm, w_branch_attn, w_branch_hgrn,
              w_out, ffn2_w_gu, ffn2_w_down):
    B = x_prompt.shape[0]
    DS = x_sample.shape[1]
    past = page_table.shape[1] * PAGE_SIZE
    meta = jnp.broadcast_to(meta_tokens[None].astype(x_prompt.dtype), (B, N_META, D_MODEL))
    xp = jnp.concatenate([meta, x_prompt], axis=1)
    topk_p = min(TOPK_MAX, xp.shape[1] // 4)
    topk_s = min(TOPK_MAX, (past + DS) // 4)
    lb_sm = jax.nn.softmax(lb_raw.astype(jnp.float32), axis=0)
    lower_bounds = jnp.cumsum(lb_sm, axis=0) - lb_sm[0]
    xs = x_sample
    rows_p, rows_s = [], []
    for l in range(DEPTH):
        w = (norm_gains[l], ffn1_w_gu[l], ffn1_w_down[l], w_in[l], lower_bounds[l], hg_norm[l],
             w_branch_attn[l], w_branch_hgrn[l], w_out[l], ffn2_w_gu[l], ffn2_w_down[l])
        attn_p = functools.partial(dsa_prompt, topk=topk_p)
        attn_s = functools.partial(dsa_sample, cache_k=cache_k, cache_v=cache_v, cache_kidx=cache_kidx,
                                   page_table=page_table, layer=l, topk=topk_s)
        hgrn_s = functools.partial(hgrn2_chunk_scan, S0=state_hgrn[l], chunk=DS)
        xp, rp = layer_forward(xp, *w, attn_p, hgrn_prompt)
        xs, rs = layer_forward(xs, *w, attn_s, hgrn_s)
        rows_p.append(rp)
        rows_s.append(rs)
    y_prompt = xp[:, N_META:]
    k_prompt = jnp.stack([r[0] for r in rows_p])
    v_prompt = jnp.stack([r[1] for r in rows_p])
    kidx_prompt = jnp.stack([r[2] for r in rows_p])
    state_hgrn_prompt = jnp.stack([r[3] for r in rows_p])
    k_sample = jnp.stack([r[0] for r in rows_s])
    v_sample = jnp.stack([r[1] for r in rows_s])
    kidx_sample = jnp.stack([r[2] for r in rows_s])
    state_hgrn_sample = jnp.stack([r[3] for r in rows_s])
    return (y_prompt, xs, k_prompt, v_prompt, kidx_prompt, state_hgrn_prompt,
            k_sample, v_sample, kidx_sample, state_hgrn_sample)
```

```python
import functools

import jax
import jax.numpy as jnp
from jax import lax
from jax.experimental import pallas as pl
from jax.experimental.pallas import tpu as pltpu

F32, BF16, I32 = jnp.float32, jnp.bfloat16, jnp.int32

D_MODEL = 1024
N_META = 16
N_HEADS = 8
HEAD_DIM = 64
ATT_W = N_HEADS * HEAD_DIM
IDX_HEADS = 8
IDX_DIM = 64
TOPK = 256
HG_HEADS = 4
HG_DK = 128
HG_DV = 128
HG_W = HG_HEADS * HG_DK
D_FF = 2816
PAGE = 128
RMS_EPS = 1e-6
NEG_INF = -1e30
LB_FLOOR = 1e-30
INT_MIN = -(2 ** 31)

LANES = 128
SUBLANES = 8
VMEM_LIMIT = 56 * 1024 * 1024

Z_Q, Z_K, Z_V, Z_QI = 0, 512, 1024, 1536
Z_HQ, Z_HF, Z_HV, Z_HG = 2048, 2560, 3072, 3584
Z_GA, Z_GH = 4096, 5120
Z_KI, Z_WI = 6144, 6208
Z_W = 6272
ORIG_OFFS = dict(q=0, k=512, v=1024, qi=1536, wi=2048, ki=2056, hq=2120, hf=2632, hv=3144,
                 hg=3656, ga=4168, gh=5192)

QB = 128
KV_STEP = 512
HG_CHUNK = 128


def _dot(a, b):
    return jnp.dot(a, b, preferred_element_type=F32)


def _dot_nt(a, b):
    return lax.dot_general(a, b, (((1,), (1,)), ((), ())), preferred_element_type=F32)


def _dot_tn(a, b):
    return lax.dot_general(a, b, (((0,), (0,)), ((), ())), preferred_element_type=F32)


def _rms(x, g):
    ms = jnp.mean(x * x, axis=-1, keepdims=True)
    return x * lax.rsqrt(ms + RMS_EPS) * g


def _sigmoid(x):
    return 1.0 / (1.0 + jnp.exp(-x))


def _params(sem):
    return pltpu.CompilerParams(dimension_semantics=sem, vmem_limit_bytes=VMEM_LIMIT)


def _ffn_body(x_ref, gpre_ref, gpost_ref, wg_ref, wu_ref, wd_ref, o_ref, h_ref, acc_ref):
    j = pl.program_id(1)

    @pl.when(j == 0)
    def _():
        h_ref[...] = _rms(x_ref[...], gpre_ref[...]).astype(BF16)
        acc_ref[...] = jnp.zeros_like(acc_ref)

    h = h_ref[...]
    g = _dot(h, wg_ref[...])
    u = _dot(h, wu_ref[...])
    a = (g * _sigmoid(g) * u).astype(BF16)
    acc_ref[...] += _dot(a, wd_ref[...])

    @pl.when(j == pl.num_programs(1) - 1)
    def _():
        o_ref[...] = x_ref[...] + 0.5 * _rms(acc_ref[...], gpost_ref[...])


def ffn(x, gpre, gpost, w_gu, w_dn, *, tm, tf):
    rows, d = x.shape
    ff = w_dn.shape[0]
    nj = ff // tf
    return pl.pallas_call(
        _ffn_body,
        grid=(rows // tm, nj),
        in_specs=[
            pl.BlockSpec((tm, d), lambda i, j: (i, 0)),
            pl.BlockSpec((1, d), lambda i, j: (0, 0)),
            pl.BlockSpec((1, d), lambda i, j: (0, 0)),
            pl.BlockSpec((d, tf), lambda i, j: (0, j)),
            pl.BlockSpec((d, tf), lambda i, j: (0, j + nj)),
            pl.BlockSpec((tf, d), lambda i, j: (j, 0)),
        ],
        out_specs=pl.BlockSpec((tm, d), lambda i, j: (i, 0)),
        out_shape=jax.ShapeDtypeStruct((rows, d), F32),
        scratch_shapes=[pltpu.VMEM((tm, d), BF16), pltpu.VMEM((tm, d), F32)],
        compiler_params=_params(("parallel", "arbitrary")),
        name="ffn",
    )(x, gpre, gpost, w_gu, w_gu, w_dn)


def _inproj_body(x_ref, g_ref, w_ref, o_ref, h_ref):
    @pl.when(pl.program_id(1) == 0)
    def _():
        h_ref[...] = _rms(x_ref[...], g_ref[...]).astype(BF16)

    o_ref[...] = _dot(h_ref[...], w_ref[...])


def inproj(x, g, w, *, tm, tn):
    rows, d = x.shape
    n = w.shape[1]
    return pl.pallas_call(
        _inproj_body,
        grid=(rows // tm, n // tn),
        in_specs=[
            pl.BlockSpec((tm, d), lambda i, j: (i, 0)),
            pl.BlockSpec((1, d), lambda i, j: (0, 0)),
            pl.BlockSpec((d, tn), lambda i, j: (0, j)),
        ],
        out_specs=pl.BlockSpec((tm, tn), lambda i, j: (i, j)),
        out_shape=jax.ShapeDtypeStruct((rows, n), F32),
        scratch_shapes=[pltpu.VMEM((tm, d), BF16)],
        compiler_params=_params(("parallel", "arbitrary")),
        name="inproj",
    )(x, g, w)


def _combine_body(x_ref, ya_ref, yh_ref, ga_ref, gh_ref, wpa_ref, wph_ref, wo_ref, g_ref, o_ref):
    a = _dot(ya_ref[...].astype(BF16), wpa_ref[...])
    hh = _dot(yh_ref[...].astype(BF16), wph_ref[...])
    m = _sigmoid(ga_ref[...]) * a + _sigmoid(gh_ref[...]) * hh
    r = _dot(m.astype(BF16), wo_ref[...])
    o_ref[...] = x_ref[...] + _rms(r, g_ref[...])


def combine(x, ya, yh, z, w_pa, w_ph, w_o, g, *, tm):
    rows, d = x.shape
    return pl.pallas_call(
        _combine_body,
        grid=(rows // tm,),
        in_specs=[
            pl.BlockSpec((tm, d), lambda i: (i, 0)),
            pl.BlockSpec((tm, ATT_W), lambda i: (i, 0)),
            pl.BlockSpec((tm, HG_W), lambda i: (i, 0)),
            pl.BlockSpec((tm, d), lambda i: (i, Z_GA // D_MODEL)),
            pl.BlockSpec((tm, d), lambda i: (i, Z_GH // D_MODEL)),
            pl.BlockSpec((ATT_W, d), lambda i: (0, 0)),
            pl.BlockSpec((HG_W, d), lambda i: (0, 0)),
            pl.BlockSpec((d, d), lambda i: (0, 0)),
            pl.BlockSpec((1, d), lambda i: (0, 0)),
        ],
        out_specs=pl.BlockSpec((tm, d), lambda i: (i, 0)),
        out_shape=jax.ShapeDtypeStruct((rows, d), F32),
        compiler_params=_params(("parallel",)),
        name="combine",
    )(x, ya, yh, z, z, w_pa, w_ph, w_o, g)


def _sortable_key(score, adm):
    s = jnp.where(score == 0.0, 0.0, score)
    b = lax.bitcast_convert_type(s, I32)
    key = b ^ ((b >> 31) & 0x7FFFFFFF)
    return jnp.where(adm, key, INT_MIN)


def _topk_select(key_ref, lk, kpos, topk, idx_bits):
    rows = key_ref.shape[0]
    kf = float(topk)

    def count(pred):
        return jnp.sum(pred.astype(F32), axis=1, keepdims=True)

    def value_step(it, thr):
        cand = thr ^ jnp.left_shift(jnp.int32(1), 31 - it)
        return jnp.where(count(key_ref[:, :lk] >= cand) >= kf, cand, thr)

    thr = lax.fori_loop(0, 32, value_step, jnp.full((rows, 1), INT_MIN, I32))
    key = key_ref[:, :lk]
    need = kf - count(key > thr)

    def index_step(it, bound):
        cand = bound | jnp.left_shift(jnp.int32(1), idx_bits - 1 - it)
        c = count((key_ref[:, :lk] == thr) & (kpos < cand))
        return jnp.where(c <= need, cand, bound)

    bound = lax.fori_loop(0, idx_bits, index_step, jnp.zeros((rows, 1), I32))
    sel = (key > thr) | ((key == thr) & (kpos < bound))
    sel = sel & (key != INT_MIN)
    return sel.astype(F32)


def _attn_prompt_variant(lk, q_ref, qi_ref, kwq_ref, k_ref, v_ref, kwk_ref, o_ref, key_ref, selm_ref,
                         topk, idx_bits):
    i = pl.program_id(1)
    qpos = i * QB + lax.broadcasted_iota(I32, (QB, 1), 0)
    kpos = lax.broadcasted_iota(I32, (1, lk), 1)
    adm = kpos <= qpos
    w8 = kwq_ref[:, IDX_DIM:IDX_DIM + IDX_HEADS] * (IDX_HEADS ** -0.5)
    ki = kwk_ref[0:lk, 0:IDX_DIM].astype(BF16)
    qi = qi_ref[...].astype(BF16)
    sc = jnp.zeros((QB, lk), F32)
    for h in range(IDX_HEADS):
        d = _dot_nt(qi[:, h * IDX_DIM:(h + 1) * IDX_DIM], ki)
        sc = sc + w8[:, h:h + 1] * jnp.maximum(d * (IDX_DIM ** -0.5), 0.0)
    key_ref[:, :lk] = _sortable_key(sc, adm)
    selm_ref[:, :lk] = _topk_select(key_ref, lk, kpos, topk, idx_bits)

    q = q_ref[...].astype(BF16)
    for h in range(N_HEADS):
        hs = slice(h * HEAD_DIM, (h + 1) * HEAD_DIM)
        kh = k_ref[0:lk, hs].astype(BF16)
        vh = v_ref[0:lk, hs].astype(BF16)
        s = _dot_nt(q[:, hs], kh) * (HEAD_DIM ** -0.5)
        s = jnp.where(selm_ref[:, :lk] > 0.5, s, NEG_INF)
        m = jnp.max(s, axis=1, keepdims=True)
        p = jnp.exp(s - m)
        l = jnp.sum(p, axis=1, keepdims=True)
        o_ref[:, hs] = _dot(p.astype(BF16), vh) / l


def _attn_prompt_body(q_ref, qi_ref, kwq_ref, k_ref, v_ref, kwk_ref, o_ref, key_ref, selm_ref, *,
                      t_pad, topk, idx_bits):
    i = pl.program_id(1)
    nqb = t_pad // QB
    per = KV_STEP // QB
    lo = 0
    while lo < nqb:
        hi = min(lo + per, nqb)
        lk = hi * QB

        @pl.when((i >= lo) & (i < hi))
        def _(lk=lk):
            _attn_prompt_variant(lk, q_ref, qi_ref, kwq_ref, k_ref, v_ref, kwk_ref, o_ref, key_ref,
                                 selm_ref, topk, idx_bits)

        lo = hi


def attn_prompt(z, *, n_seq, t_pad, topk):
    nqb = t_pad // QB
    idx_bits = max(1, (t_pad - 1).bit_length())
    cq, ck, cv, cqi = Z_Q // ATT_W, Z_K // ATT_W, Z_V // ATT_W, Z_QI // ATT_W
    ckw = Z_KI // LANES
    body = functools.partial(_attn_prompt_body, t_pad=t_pad, topk=topk, idx_bits=idx_bits)
    return pl.pallas_call(
        body,
        grid=(n_seq, nqb),
        in_specs=[
            pl.BlockSpec((QB, ATT_W), lambda b, i: (b * nqb + i, cq)),
            pl.BlockSpec((QB, ATT_W), lambda b, i: (b * nqb + i, cqi)),
            pl.BlockSpec((QB, LANES), lambda b, i: (b * nqb + i, ckw)),
            pl.BlockSpec((t_pad, ATT_W), lambda b, i: (b, ck)),
            pl.BlockSpec((t_pad, ATT_W), lambda b, i: (b, cv)),
            pl.BlockSpec((t_pad, LANES), lambda b, i: (b, ckw)),
        ],
        out_specs=pl.BlockSpec((QB, ATT_W), lambda b, i: (b * nqb + i, 0)),
        out_shape=jax.ShapeDtypeStruct((n_seq * t_pad, ATT_W), F32),
        scratch_shapes=[pltpu.VMEM((QB, t_pad), I32), pltpu.VMEM((QB, t_pad), F32)],
        compiler_params=_params(("parallel", "arbitrary")),
        name="attn_prompt",
    )(z, z, z, z, z, z)


IDX_PAGES = 16
KV_PAGES = 8


def _attn_sample_body(pt_ref, q_ref, qi_ref, kwn_ref, kn_ref, vn_ref, ckidx_hbm, ck_hbm, cv_hbm, o_ref,
                      kibuf, kbuf, vbuf, sc_ref, key_ref, selm_ref, sem_ki, sem_k, sem_v, *,
                      layer, n_pages, ds, topk, idx_bits):
    b = pl.program_id(0)
    past = n_pages * PAGE
    ltot = past + LANES
    n_ch1 = n_pages // IDX_PAGES
    n_ch2 = n_pages // KV_PAGES
    rows_h = N_HEADS * ds

    def ki_copy(c, slot, p):
        page = pt_ref[b, c * IDX_PAGES + p]
        return pltpu.make_async_copy(ckidx_hbm.at[layer, page], kibuf.at[slot, p], sem_ki.at[slot])

    def kv_copy(c, slot, p):
        page = pt_ref[b, c * KV_PAGES + p]
        return (pltpu.make_async_copy(ck_hbm.at[layer, page], kbuf.at[slot, p], sem_k.at[slot]),
                pltpu.make_async_copy(cv_hbm.at[layer, page], vbuf.at[slot, p], sem_v.at[slot]))

    def start_ki(c, slot):
        for p in range(IDX_PAGES):
            ki_copy(c, slot, p).start()

    def wait_ki(c, slot):
        for p in range(IDX_PAGES):
            ki_copy(c, slot, p).wait()

    def start_kv(c, slot):
        for p in range(KV_PAGES):
            ck, cv = kv_copy(c, slot, p)
            ck.start()
            cv.start()

    def wait_kv(c, slot):
        for p in range(KV_PAGES):
            ck, cv = kv_copy(c, slot, p)
            ck.wait()
            cv.wait()

    start_ki(0, 0)
    start_kv(0, 0)

    qi = qi_ref[...]
    qh = jnp.concatenate([qi[:, h * IDX_DIM:(h + 1) * IDX_DIM] for h in range(IDX_HEADS)], axis=0).astype(BF16)
    w8 = kwn_ref[:, IDX_DIM:IDX_DIM + IDX_HEADS] * (IDX_HEADS ** -0.5)
    wcol = jnp.concatenate([w8[:, h:h + 1] for h in range(IDX_HEADS)], axis=0)

    def idx_scores(d):
        r = wcol * jnp.maximum(d * (IDX_DIM ** -0.5), 0.0)
        acc = r[0:ds]
        for h in range(1, IDX_HEADS):
            acc = acc + r[h * ds:(h + 1) * ds]
        return acc

    def idx_step(c, carry):
        slot = c % 2

        @pl.when(c + 1 < n_ch1)
        def _():
            start_ki(c + 1, 1 - slot)

        wait_ki(c, slot)
        width = IDX_PAGES * PAGE
        kit = jnp.concatenate([kibuf[slot, p] for p in range(IDX_PAGES)], axis=1).astype(BF16)
        sc_ref[:, pl.ds(pl.multiple_of(c * width, width), width)] = idx_scores(_dot(qh, kit))
        return carry

    lax.fori_loop(0, n_ch1, idx_step, 0)
    ki_new = jnp.concatenate([kwn_ref[:, 0:IDX_DIM], jnp.zeros((LANES - ds, IDX_DIM), F32)], axis=0)
    sc_ref[:, past:ltot] = idx_scores(_dot_nt(qh, ki_new.astype(BF16)))

    qpos = past + lax.broadcasted_iota(I32, (ds, 1), 0)
    kpos = lax.broadcasted_iota(I32, (1, ltot), 1)
    key_ref[...] = _sortable_key(sc_ref[...], kpos <= qpos)
    selm_ref[...] = _topk_select(key_ref, ltot, kpos, topk, idx_bits)

    q8 = jnp.concatenate([q_ref[...]] * N_HEADS, axis=0)
    rhead = lax.broadcasted_iota(I32, (rows_h, ATT_W), 0) // ds
    chead = lax.broadcasted_iota(I32, (rows_h, ATT_W), 1) // HEAD_DIM
    qbd = jnp.where(rhead == chead, q8, 0.0).astype(BF16)

    def attend(s, msk, pv_fn, state):
        m_old, l_old, acc = state
        on = jnp.concatenate([msk] * N_HEADS, axis=0) > 0.5
        s = jnp.where(on, s * (HEAD_DIM ** -0.5), NEG_INF)
        m_new = jnp.maximum(m_old, jnp.max(s, axis=1, keepdims=True))
        alpha = jnp.exp(m_old - m_new)
        p = jnp.where(on, jnp.exp(s - m_new), 0.0)
        l_new = alpha * l_old + jnp.sum(p, axis=1, keepdims=True)
        return m_new, l_new, alpha * acc + pv_fn(p.astype(BF16))

    def kv_step(c, state):
        slot = c % 2

        @pl.when(c + 1 < n_ch2)
        def _():
            start_kv(c + 1, 1 - slot)

        wait_kv(c, slot)
        width = KV_PAGES * PAGE
        s = jnp.concatenate(
            [_dot(qbd, kbuf[slot, p].reshape(ATT_W, PAGE).astype(BF16)) for p in range(KV_PAGES)], axis=1)

        def pv_fn(p):
            acc = _dot_nt(p[:, 0:PAGE], vbuf[slot, 0].reshape(ATT_W, PAGE).astype(BF16))
            for j in range(1, KV_PAGES):
                acc = acc + _dot_nt(p[:, j * PAGE:(j + 1) * PAGE],
                                    vbuf[slot, j].reshape(ATT_W, PAGE).astype(BF16))
            return acc

        msk = selm_ref[:, pl.ds(pl.multiple_of(c * width, width), width)]
        return attend(s, msk, pv_fn, state)

    state = (jnp.full((rows_h, 1), NEG_INF, F32), jnp.zeros((rows_h, 1), F32),
             jnp.zeros((rows_h, ATT_W), F32))
    state = lax.fori_loop(0, n_ch2, kv_step, state)
    pad = jnp.zeros((LANES - ds, ATT_W), F32)
    kn = jnp.concatenate([kn_ref[...], pad], axis=0).astype(BF16)
    vn = jnp.concatenate([vn_ref[...], pad], axis=0).astype(BF16)
    _, l_fin, acc = attend(_dot_nt(qbd, kn), selm_ref[:, past:ltot], lambda p: _dot(p, vn), state)
    out = acc / l_fin
    for h in range(N_HEADS):
        hs = slice(h * HEAD_DIM, (h + 1) * HEAD_DIM)
        o_ref[:, hs] = out[h * ds:(h + 1) * ds, hs]


def attn_sample(z, page_table, cache_kidx, cache_k, cache_v, *, layer, row0, n_seq, ds, topk):
    n_pages = page_table.shape[1]
    past = n_pages * PAGE
    ltot = past + LANES
    idx_bits = max(1, (ltot - 1).bit_length())
    rb = row0 // ds
    cq, ck, cv, cqi = Z_Q // ATT_W, Z_K // ATT_W, Z_V // ATT_W, Z_QI // ATT_W
    ckw = Z_KI // LANES
    body = functools.partial(_attn_sample_body, layer=layer, n_pages=n_pages, ds=ds, topk=topk,
                             idx_bits=idx_bits)
    grid_spec = pltpu.PrefetchScalarGridSpec(
        num_scalar_prefetch=1,
        grid=(n_seq,),
        in_specs=[
            pl.BlockSpec((ds, ATT_W), lambda b, pt: (rb + b, cq)),
            pl.BlockSpec((ds, ATT_W), lambda b, pt: (rb + b, cqi)),
            pl.BlockSpec((ds, LANES), lambda b, pt: (rb + b, ckw)),
            pl.BlockSpec((ds, ATT_W), lambda b, pt: (rb + b, ck)),
            pl.BlockSpec((ds, ATT_W), lambda b, pt: (rb + b, cv)),
            pl.BlockSpec(memory_space=pl.ANY),
            pl.BlockSpec(memory_space=pl.ANY),
            pl.BlockSpec(memory_space=pl.ANY),
        ],
        out_specs=pl.BlockSpec((ds, ATT_W), lambda b, pt: (b, 0)),
        scratch_shapes=[
            pltpu.VMEM((2, IDX_PAGES, IDX_DIM, PAGE), F32),
            pltpu.VMEM((2, KV_PAGES, N_HEADS, HEAD_DIM, PAGE), F32),
            pltpu.VMEM((2, KV_PAGES, N_HEADS, HEAD_DIM, PAGE), F32),
            pltpu.VMEM((ds, ltot), F32),
            pltpu.VMEM((ds, ltot), I32),
            pltpu.VMEM((ds, ltot), F32),
            pltpu.SemaphoreType.DMA((2,)),
            pltpu.SemaphoreType.DMA((2,)),
            pltpu.SemaphoreType.DMA((2,)),
        ],
    )
    return pl.pallas_call(
        body,
        grid_spec=grid_spec,
        out_shape=jax.ShapeDtypeStruct((n_seq * ds, ATT_W), F32),
        compiler_params=_params(("arbitrary",)),
        name="attn_sample",
    )(page_table, z, z, z, z, z, cache_kidx, cache_k, cache_v)


def _hgrn_chunk(zq, zf, vv, zg, valid, lbf, gn, st_ref):
    c = zq.shape[0]
    levels = c.bit_length() - 1
    sig = _sigmoid(zf)
    f = lbf + (1.0 - lbf) * sig
    logf = jnp.where(valid, jnp.log(f), 0.0)
    kk = jnp.where(valid, (1.0 - lbf) * _sigmoid(-zf), 0.0)
    qq = zq * _sigmoid(zq)
    vb = jnp.where(valid, vv, 0.0).astype(BF16)

    tq = lax.broadcasted_iota(I32, (c, 1), 0)
    ts = lax.broadcasted_iota(I32, (1, c), 1)
    tril = (ts <= tq).astype(BF16)
    hi = logf.astype(BF16)
    r1 = logf - hi.astype(F32)
    mid = r1.astype(BF16)
    lo = (r1 - mid.astype(F32)).astype(BF16)
    bcum = _dot(tril, hi) + _dot(tril, mid) + _dot(tril, lo)

    x = tq ^ ts
    heads = [slice(h * HG_DK, (h + 1) * HG_DK) for h in range(HG_HEADS)]
    qb0, kb0 = qq.astype(BF16), kk.astype(BF16)
    amat = [jnp.where(x == 0, _dot_nt(qb0[:, hs], kb0[:, hs]), 0.0) for hs in heads]
    gprev = jnp.where(tq == 0, 0.0, pltpu.roll(bcum, 1, 0))
    hend = bcum
    for lev in range(levels):
        m = 1 << lev
        qs = (qq * jnp.exp(bcum - gprev)).astype(BF16)
        ks = (kk * jnp.exp(hend - bcum)).astype(BF16)
        sib = ((x >> lev) == 1) & (ts < tq)
        for h, hs in enumerate(heads):
            amat[h] = amat[h] + jnp.where(sib, _dot_nt(qs[:, hs], ks[:, hs]), 0.0)
        upper = ((tq >> lev) & 1) == 1
        gprev = jnp.where(upper, pltpu.roll(gprev, m, 0), gprev)
        hend = jnp.where(upper, hend, pltpu.roll(hend, c - m, 0))
    qs = (qq * jnp.exp(bcum)).astype(BF16)
    ks = (kk * jnp.exp(hend - bcum)).astype(BF16)
    decay = jnp.exp(hend[0:1, :])
    outs = []
    for h, hs in enumerate(heads):
        st = st_ref[h]
        o = _dot_nt(qs[:, hs], st.astype(BF16)) + _dot(amat[h].astype(BF16), vb[:, hs])
        st_ref[h] = st * decay[:, hs] + _dot_tn(vb[:, hs], ks[:, hs])
        ms = jnp.mean(o * o, axis=-1, keepdims=True)
        g = zg[:, hs]
        outs.append(o * lax.rsqrt(ms + RMS_EPS) * gn * (g * _sigmoid(g)))
    return jnp.concatenate(outs, axis=1)


def _lower_bound(lbraw_ref, layer):
    r = lbraw_ref[...]
    e = jnp.exp(r - jnp.max(r, axis=0, keepdims=True))
    sm = e / jnp.sum(e, axis=0, keepdims=True)
    cs = sm[0:1]
    for l in range(1, layer + 1):
        cs = cs + sm[l:l + 1]
    return jnp.maximum(cs - sm[0:1], LB_FLOOR)


def _hgrn_body(hq_ref, hf_ref, hv_ref, hg_ref, lbraw_ref, gn_ref, s0_ref, y_ref, sout_ref, st_ref, *,
               layer, rows_in, n_valid):
    c = HG_CHUNK
    lbf = _lower_bound(lbraw_ref, layer)
    gn = gn_ref[...]
    for h in range(HG_HEADS):
        st_ref[h] = s0_ref[0, h].T
    tq = lax.broadcasted_iota(I32, (c, 1), 0)
    if rows_in < c:
        pad = jnp.zeros((c - rows_in, HG_W), F32)

        def padded(ref):
            return jnp.concatenate([ref[...], pad], axis=0)

        y = _hgrn_chunk(padded(hq_ref), padded(hf_ref), padded(hv_ref), padded(hg_ref), tq < n_valid, lbf,
                        gn, st_ref)
        y_ref[...] = y[0:rows_in]
    else:
        def step(ci, carry):
            rows = pl.ds(pl.multiple_of(ci * c, c), c)
            y_ref[rows, :] = _hgrn_chunk(hq_ref[rows, :], hf_ref[rows, :], hv_ref[rows, :], hg_ref[rows, :],
                                         (ci * c + tq) < n_valid, lbf, gn, st_ref)
            return carry

        lax.fori_loop(0, rows_in // c, step, 0)
    for h in range(HG_HEADS):
        sout_ref[0, h] = st_ref[h].T


def hgrn(z, lb_raw, hg_norm, s0, *, layer, row0, n_seq, rows_in, n_valid):
    rb = row0 // rows_in
    depth = lb_raw.shape[0]
    body = functools.partial(_hgrn_body, layer=layer, rows_in=rows_in, n_valid=n_valid)
    cols = [Z_HQ // HG_W, Z_HF // HG_W, Z_HV // HG_W, Z_HG // HG_W]
    zspec = [pl.BlockSpec((rows_in, HG_W), functools.partial(lambda b, cc: (rb + b, cc), cc=cc)) for cc in cols]
    return pl.pallas_call(
        body,
        grid=(n_seq,),
        in_specs=zspec + [
            pl.BlockSpec((depth, HG_W), lambda b: (0, 0)),
            pl.BlockSpec((1, HG_DV), lambda b: (0, 0)),
            pl.BlockSpec((1, HG_HEADS, HG_DK, HG_DV), lambda b: (b, 0, 0, 0)),
        ],
        out_specs=[
            pl.BlockSpec((rows_in, HG_W), lambda b: (b, 0)),
            pl.BlockSpec((1, HG_HEADS, HG_DK, HG_DV), lambda b: (b, 0, 0, 0)),
        ],
        out_shape=[
            jax.ShapeDtypeStruct((n_seq * rows_in, HG_W), F32),
            jax.ShapeDtypeStruct((n_seq, HG_HEADS, HG_DK, HG_DV), F32),
        ],
        scratch_shapes=[pltpu.VMEM((HG_HEADS, HG_DV, HG_DK), F32)],
        compiler_params=_params(("parallel",)),
        name="hgrn",
    )(z, z, z, z, lb_raw, hg_norm, s0)


def _permute_w_in(w):
    o = ORIG_OFFS
    parts = [w[:, o["q"]:o["q"] + 2048],
             w[:, o["hq"]:o["hq"] + 2048],
             w[:, o["ga"]:o["ga"] + 2048],
             w[:, o["ki"]:o["ki"] + IDX_DIM],
             w[:, o["wi"]:o["wi"] + IDX_HEADS],
             jnp.zeros((w.shape[0], Z_W - Z_WI - IDX_HEADS), w.dtype)]
    return jnp.concatenate(parts, axis=1).astype(BF16)


ROW_TILE = 768


def kernel(x_prompt, x_sample, cache_k, cache_v, cache_kidx, state_hgrn, page_table, meta_tokens, norm_gains,
           ffn1_w_gu, ffn1_w_down, w_in, lb_raw, hg_norm, w_branch_attn, w_branch_hgrn, w_out, ffn2_w_gu,
           ffn2_w_down):
    return _forward(x_prompt, x_sample, cache_k, cache_v, cache_kidx, state_hgrn, page_table, meta_tokens,
                    norm_gains, ffn1_w_gu, ffn1_w_down, w_in, lb_raw, hg_norm, w_branch_attn, w_branch_hgrn,
                    w_out, ffn2_w_gu, ffn2_w_down, tm=ROW_TILE)


def _forward(x_prompt, x_sample, cache_k, cache_v, cache_kidx, state_hgrn, page_table, meta_tokens, norm_gains,
             ffn1_w_gu, ffn1_w_down, w_in, lb_raw, hg_norm, w_branch_attn, w_branch_hgrn, w_out, ffn2_w_gu,
             ffn2_w_down, *, tm):
    n_seq, seq, d = x_prompt.shape
    db, ds, _ = x_sample.shape
    depth = w_in.shape[0]
    t_real = seq + N_META
    t_pad = -(-t_real // QB) * QB
    rows_p = n_seq * t_pad
    rows = rows_p + db * ds
    assert rows % tm == 0
    topk_p = min(TOPK, t_real // 4)
    topk_s = min(TOPK, (page_table.shape[1] * PAGE + ds) // 4)

    meta = jnp.broadcast_to(meta_tokens[None].astype(F32), (n_seq, N_META, d))
    xp = jnp.concatenate([meta, x_prompt, jnp.zeros((n_seq, t_pad - t_real, d), F32)], axis=1)
    x = jnp.concatenate([xp.reshape(rows_p, d), x_sample.reshape(db * ds, d)], axis=0)

    ck = cache_k.transpose(0, 1, 3, 4, 2)
    cv = cache_v.transpose(0, 1, 3, 4, 2)
    cki = cache_kidx.transpose(0, 1, 3, 2)
    zero_state = jnp.zeros((n_seq, HG_HEADS, HG_DK, HG_DV), F32)
    hgn = hg_norm.reshape(depth, 1, HG_DV)

    outs = {k: [] for k in ("kp", "vp", "kip", "sp", "ks", "vs", "kis", "ss")}
    for l in range(depth):
        g = norm_gains[l].reshape(6, 1, d)
        x = ffn(x, g[0], g[1], ffn1_w_gu[l].astype(BF16), ffn1_w_down[l].astype(BF16), tm=tm, tf=256)
        z = inproj(x, g[2], _permute_w_in(w_in[l]), tm=tm, tn=896)
        ya_p = attn_prompt(z, n_seq=n_seq, t_pad=t_pad, topk=topk_p)
        ya_s = attn_sample(z, page_table, cki, ck, cv, layer=l, row0=rows_p, n_seq=db, ds=ds,
                           topk=topk_s)
        yh_p, st_p = hgrn(z, lb_raw, hgn[l], zero_state, layer=l, row0=0, n_seq=n_seq, rows_in=t_pad,
                          n_valid=t_real)
        yh_s, st_s = hgrn(z, lb_raw, hgn[l], state_hgrn[l], layer=l, row0=rows_p, n_seq=db, rows_in=ds,
                          n_valid=ds)
        ya = jnp.concatenate([ya_p, ya_s], axis=0)
        yh = jnp.concatenate([yh_p, yh_s], axis=0)
        x = combine(x, ya, yh, z, w_branch_attn[l].astype(BF16), w_branch_hgrn[l].astype(BF16),
                    w_out[l].astype(BF16), g[3], tm=tm)
        x = ffn(x, g[4], g[5], ffn2_w_gu[l].astype(BF16), ffn2_w_down[l].astype(BF16), tm=tm, tf=256)

        zp = z[:rows_p].reshape(n_seq, t_pad, Z_W)[:, :t_real]
        zs = z[rows_p:].reshape(db, ds, Z_W)
        outs["kp"].append(zp[..., Z_K:Z_K + ATT_W].reshape(n_seq, t_real, N_HEADS, HEAD_DIM))
        outs["vp"].append(zp[..., Z_V:Z_V + ATT_W].reshape(n_seq, t_real, N_HEADS, HEAD_DIM))
        outs["kip"].append(zp[..., Z_KI:Z_KI + IDX_DIM])
        outs["sp"].append(st_p)
        outs["ks"].append(zs[..., Z_K:Z_K + ATT_W].reshape(db, ds, N_HEADS, HEAD_DIM))
        outs["vs"].append(zs[..., Z_V:Z_V + ATT_W].reshape(db, ds, N_HEADS, HEAD_DIM))
        outs["kis"].append(zs[..., Z_KI:Z_KI + IDX_DIM])
        outs["ss"].append(st_s)

    y_prompt = x[:rows_p].reshape(n_seq, t_pad, d)[:, N_META:t_real]
    y_sample = x[rows_p:].reshape(db, ds, d)
    st = lambda k: jnp.stack(outs[k])
    return (y_prompt, y_sample, st("kp"), st("vp"), st("kip"), st("sp"), st("ks"), st("vs"), st("kis"),
            st("ss"))
```

```python
import functools

import jax
import jax.numpy as jnp
from jax import lax
from jax.experimental import pallas as pl
from jax.experimental.pallas import tpu as pltpu

F32, BF16, I32 = jnp.float32, jnp.bfloat16, jnp.int32

D_MODEL = 1024
N_META = 16
N_HEADS = 8
HEAD_DIM = 64
ATT_W = N_HEADS * HEAD_DIM
IDX_HEADS = 8
IDX_DIM = 64
TOPK = 256
HG_HEADS = 4
HG_DK = 128
HG_DV = 128
HG_W = HG_HEADS * HG_DK
D_FF = 2816
PAGE = 128
RMS_EPS = 1e-6
NEG_INF = -1e30
LB_FLOOR = 1e-30
INT_MIN = -(2 ** 31)

LANES = 128
SUBLANES = 8
VMEM_LIMIT = 56 * 1024 * 1024

Z_Q, Z_K, Z_V, Z_QI = 0, 512, 1024, 1536
Z_HQ, Z_HF, Z_HV, Z_HG = 2048, 2560, 3072, 3584
Z_GA, Z_GH = 4096, 5120
Z_KI, Z_WI = 6144, 6208
Z_W = 6272
ORIG_OFFS = dict(q=0, k=512, v=1024, qi=1536, wi=2048, ki=2056, hq=2120, hf=2632, hv=3144,
                 hg=3656, ga=4168, gh=5192)

QB = 128
KV_STEP = 512
HG_CHUNK = 128


def _dot(a, b):
    return jnp.dot(a, b, preferred_element_type=F32)


def _dot_nt(a, b):
    return lax.dot_general(a, b, (((1,), (1,)), ((), ())), preferred_element_type=F32)


def _dot_tn(a, b):
    return lax.dot_general(a, b, (((0,), (0,)), ((), ())), preferred_element_type=F32)


def _rms(x, g):
    ms = jnp.mean(x * x, axis=-1, keepdims=True)
    return x * lax.rsqrt(ms + RMS_EPS) * g


def _sigmoid(x):
    return 1.0 / (1.0 + jnp.exp(-x))


def _params(sem):
    return pltpu.CompilerParams(dimension_semantics=sem, vmem_limit_bytes=VMEM_LIMIT)


def _ffn_body(x_ref, gpre_ref, gpost_ref, wg_ref, wu_ref, wd_ref, o_ref, h_ref, acc_ref):
    j = pl.program_id(1)

    @pl.when(j == 0)
    def _():
        h_ref[...] = _rms(x_ref[...], gpre_ref[...]).astype(BF16)
        acc_ref[...] = jnp.zeros_like(acc_ref)

    h = h_ref[...]
    g = _dot(h, wg_ref[...])
    u = _dot(h, wu_ref[...])
    a = (g * _sigmoid(g) * u).astype(BF16)
    acc_ref[...] += _dot(a, wd_ref[...])

    @pl.when(j == pl.num_programs(1) - 1)
    def _():
        o_ref[...] = x_ref[...] + 0.5 * _rms(acc_ref[...], gpost_ref[...])


def ffn(x, gpre, gpost, w_gu, w_dn, *, tm, tf):
    rows, d = x.shape
    ff = w_dn.shape[0]
    nj = ff // tf
    return pl.pallas_call(
        _ffn_body,
        grid=(rows // tm, nj),
        in_specs=[
            pl.BlockSpec((tm, d), lambda i, j: (i, 0)),
            pl.BlockSpec((1, d), lambda i, j: (0, 0)),
            pl.BlockSpec((1, d), lambda i, j: (0, 0)),
            pl.BlockSpec((d, tf), lambda i, j: (0, j)),
            pl.BlockSpec((d, tf), lambda i, j: (0, j + nj)),
            pl.BlockSpec((tf, d), lambda i, j: (j, 0)),
        ],
        out_specs=pl.BlockSpec((tm, d), lambda i, j: (i, 0)),
        out_shape=jax.ShapeDtypeStruct((rows, d), F32),
        scratch_shapes=[pltpu.VMEM((tm, d), BF16), pltpu.VMEM((tm, d), F32)],
        compiler_params=_params(("parallel", "arbitrary")),
        name="ffn",
    )(x, gpre, gpost, w_gu, w_gu, w_dn)


def _inproj_body(x_ref, g_ref, w_ref, o_ref, h_ref):
    @pl.when(pl.program_id(1) == 0)
    def _():
        h_ref[...] = _rms(x_ref[...], g_ref[...]).astype(BF16)

    o_ref[...] = _dot(h_ref[...], w_ref[...])


def inproj(x, g, w, *, tm, tn):
    rows, d = x.shape
    n = w.shape[1]
    return pl.pallas_call(
        _inproj_body,
        grid=(rows // tm, n // tn),
        in_specs=[
            pl.BlockSpec((tm, d), lambda i, j: (i, 0)),
            pl.BlockSpec((1, d), lambda i, j: (0, 0)),
            pl.BlockSpec((d, tn), lambda i, j: (0, j)),
        ],
        out_specs=pl.BlockSpec((tm, tn), lambda i, j: (i, j)),
        out_shape=jax.ShapeDtypeStruct((rows, n), F32),
        scratch_shapes=[pltpu.VMEM((tm, d), BF16)],
        compiler_params=_params(("parallel", "arbitrary")),
        name="inproj",
    )(x, g, w)


def _combine_body(x_ref, ya_ref, yh_ref, ga_ref, gh_ref, wpa_ref, wph_ref, wo_ref, g_ref, o_ref):
    a = _dot(ya_ref[...].astype(BF16), wpa_ref[...])
    hh = _dot(yh_ref[...].astype(BF16), wph_ref[...])
    m = _sigmoid(ga_ref[...]) * a + _sigmoid(gh_ref[...]) * hh
    r = _dot(m.astype(BF16), wo_ref[...])
    o_ref[...] = x_ref[...] + _rms(r, g_ref[...])


def combine(x, ya, yh, z, w_pa, w_ph, w_o, g, *, tm):
    rows, d = x.shape
    return pl.pallas_call(
        _combine_body,
        grid=(rows // tm,),
        in_specs=[
            pl.BlockSpec((tm, d), lambda i: (i, 0)),
            pl.BlockSpec((tm, ATT_W), lambda i: (i, 0)),
            pl.BlockSpec((tm, HG_W), lambda i: (i, 0)),
            pl.BlockSpec((tm, d), lambda i: (i, Z_GA // D_MODEL)),
            pl.BlockSpec((tm, d), lambda i: (i, Z_GH // D_MODEL)),
            pl.BlockSpec((ATT_W, d), lambda i: (0, 0)),
            pl.BlockSpec((HG_W, d), lambda i: (0, 0)),
            pl.BlockSpec((d, d), lambda i: (0, 0)),
            pl.BlockSpec((1, d), lambda i: (0, 0)),
        ],
        out_specs=pl.BlockSpec((tm, d), lambda i: (i, 0)),
        out_shape=jax.ShapeDtypeStruct((rows, d), F32),
        compiler_params=_params(("parallel",)),
        name="combine",
    )(x, ya, yh, z, z, w_pa, w_ph, w_o, g)


def _sortable_key(score, adm):
    s = jnp.where(score == 0.0, 0.0, score)
    b = lax.bitcast_convert_type(s, I32)
    key = b ^ ((b >> 31) & 0x7FFFFFFF)
    return jnp.where(adm, key, INT_MIN)


HALF_BIAS = 32768
PACK16 = 16
N_ACC = 4


def _topk_mask(key_ref, k16_ref, n, kpos, topk, idx_bits, axis):
    kf = float(topk)
    view = (lambda r: r[:n, :]) if axis == 0 else (lambda r: r[:, :n])
    tile = PACK16 if axis == 0 else LANES

    def tile_of(j):
        return k16_ref[j * tile:(j + 1) * tile, :] if axis == 0 else k16_ref[:, j * tile:(j + 1) * tile]

    def count(pred):
        return jnp.sum(jnp.where(pred, 1.0, 0.0), axis=axis, keepdims=True)

    def count16(cand16):
        accs = [None] * N_ACC
        for j in range(n // tile):
            one = jnp.where(tile_of(j) >= cand16, jnp.int16(1), jnp.int16(0))
            accs[j % N_ACC] = one if accs[j % N_ACC] is None else accs[j % N_ACC] + one
        tot = accs[0]
        for a in accs[1:]:
            if a is not None:
                tot = tot + a
        return jnp.sum(tot.astype(F32), axis=axis, keepdims=True)

    def search16(need):
        def step(it, thr_u):
            cand_u = thr_u | jnp.left_shift(jnp.int32(1), 15 - it)
            cnt = count16((cand_u - HALF_BIAS).astype(jnp.int16))
            return jnp.where(cnt >= need, cand_u, thr_u)

        shape = (1, key_ref.shape[1]) if axis == 0 else (key_ref.shape[0], 1)
        return lax.fori_loop(0, 16, step, jnp.zeros(shape, I32))

    key = view(key_ref)
    hi = key >> 16
    if axis == 0:
        k16_ref[:n, :] = hi.astype(jnp.int16)
    else:
        k16_ref[:, :n] = hi.astype(jnp.int16)
    thr_hi = search16(kf) - HALF_BIAS
    need_lo = kf - count(hi > thr_hi)
    lo = jnp.where(hi == thr_hi, (key & 0xFFFF) - HALF_BIAS, -HALF_BIAS).astype(jnp.int16)
    if axis == 0:
        k16_ref[:n, :] = lo
    else:
        k16_ref[:, :n] = lo
    thr = (thr_hi << 16) | search16(need_lo)

    gt = key > thr
    eq = key == thr
    need = kf - count(gt)
    tie = ((count(eq) > need) & (thr != INT_MIN)).astype(F32)

    def index_search():
        def step(it, bound):
            cand = bound | jnp.left_shift(jnp.int32(1), idx_bits - 1 - it)
            c = count((view(key_ref) == thr) & (kpos < cand))
            return jnp.where(c <= need, cand, bound)

        return lax.fori_loop(0, idx_bits, step, jnp.zeros(thr.shape, I32))

    bound = lax.cond(jnp.max(tie) > 0.5, index_search, lambda: jnp.full(thr.shape, 1 << idx_bits, I32))
    return (gt | (eq & (kpos < bound))) & (key != INT_MIN)


def _attn_prompt_variant(lk, q_ref, qi_ref, kwq_ref, k_ref, v_ref, kwk_ref, o_ref, key_ref, k16_ref, bias_ref,
                         topk, idx_bits):
    i = pl.program_id(1)
    qpos = i * QB + lax.broadcasted_iota(I32, (1, QB), 1)
    kpos = lax.broadcasted_iota(I32, (lk, 1), 0)
    w8t = kwq_ref[...].T[IDX_DIM:IDX_DIM + IDX_HEADS, :] * (IDX_HEADS ** -0.5)
    ki = kwk_ref[0:lk, 0:IDX_DIM].astype(BF16)
    qi = qi_ref[...].astype(BF16)
    sc = jnp.zeros((lk, QB), F32)
    for h in range(IDX_HEADS):
        d = _dot_nt(ki, qi[:, h * IDX_DIM:(h + 1) * IDX_DIM])
        sc = sc + w8t[h:h + 1, :] * jnp.maximum(d * (IDX_DIM ** -0.5), 0.0)
    key_ref[:lk, :] = _sortable_key(sc, kpos <= qpos)
    sel = _topk_mask(key_ref, k16_ref, lk, kpos, topk, idx_bits, axis=0)
    bias_ref[:, :lk] = jnp.where(sel, 0.0, NEG_INF).T

    q = (q_ref[...] * (HEAD_DIM ** -0.5)).astype(BF16)
    for h in range(N_HEADS):
        hs = slice(h * HEAD_DIM, (h + 1) * HEAD_DIM)
        kh = k_ref[0:lk, hs].astype(BF16)
        vh = v_ref[0:lk, hs].astype(BF16)
        s = _dot_nt(q[:, hs], kh) + bias_ref[:, :lk]
        m = jnp.max(s, axis=1, keepdims=True)
        p = jnp.exp(s - m)
        l = jnp.sum(p, axis=1, keepdims=True)
        o_ref[:, hs] = _dot(p.astype(BF16), vh) / l


def _attn_prompt_body(q_ref, qi_ref, kwq_ref, k_ref, v_ref, kwk_ref, o_ref, key_ref, k16_ref, bias_ref, *,
                      t_pad, topk, idx_bits):
    i = pl.program_id(1)
    nqb = t_pad // QB
    per = KV_STEP // QB
    lo = 0
    while lo < nqb:
        hi = min(lo + per, nqb)
        lk = hi * QB

        @pl.when((i >= lo) & (i < hi))
        def _(lk=lk):
            _attn_prompt_variant(lk, q_ref, qi_ref, kwq_ref, k_ref, v_ref, kwk_ref, o_ref, key_ref, k16_ref,
                                 bias_ref, topk, idx_bits)

        lo = hi


def attn_prompt(z, *, n_seq, t_pad, topk):
    nqb = t_pad // QB
    idx_bits = max(1, (t_pad - 1).bit_length())
    cq, ck, cv, cqi = Z_Q // ATT_W, Z_K // ATT_W, Z_V // ATT_W, Z_QI // ATT_W
    ckw = Z_KI // LANES
    body = functools.partial(_attn_prompt_body, t_pad=t_pad, topk=topk, idx_bits=idx_bits)
    return pl.pallas_call(
        body,
        grid=(n_seq, nqb),
        in_specs=[
            pl.BlockSpec((QB, ATT_W), lambda b, i: (b * nqb + i, cq)),
            pl.BlockSpec((QB, ATT_W), lambda b, i: (b * nqb + i, cqi)),
            pl.BlockSpec((QB, LANES), lambda b, i: (b * nqb + i, ckw)),
            pl.BlockSpec((t_pad, ATT_W), lambda b, i: (b, ck)),
            pl.BlockSpec((t_pad, ATT_W), lambda b, i: (b, cv)),
            pl.BlockSpec((t_pad, LANES), lambda b, i: (b, ckw)),
        ],
        out_specs=pl.BlockSpec((QB, ATT_W), lambda b, i: (b * nqb + i, 0)),
        out_shape=jax.ShapeDtypeStruct((n_seq * t_pad, ATT_W), F32),
        scratch_shapes=[pltpu.VMEM((t_pad, QB), I32), pltpu.VMEM((t_pad, QB), jnp.int16),
                        pltpu.VMEM((QB, t_pad), F32)],
        compiler_params=_params(("parallel", "arbitrary")),
        name="attn_prompt",
    )(z, z, z, z, z, z)


IDX_PAGES = 16
KV_PAGES = 8


SEL_SEQS = 8


def _sample_select_body(pt_ref, qi_ref, kwn_ref, ckidx_hbm, mask_ref, kibuf, sc_ref, key_ref, k16_ref, sem_ki, *,
                        layer, n_pages, ds, topk, idx_bits):
    g = pl.program_id(0)
    past = n_pages * PAGE
    ltot = past + LANES
    n_ch = n_pages // IDX_PAGES
    rows = sc_ref.shape[0]
    sel_seqs = rows // ds
    n_steps = sel_seqs * n_ch
    width = IDX_PAGES * PAGE

    def ki_copy(step, slot, p):
        seq = g * sel_seqs + step // n_ch
        page = pt_ref[seq, (step % n_ch) * IDX_PAGES + p]
        return pltpu.make_async_copy(ckidx_hbm.at[layer, page], kibuf.at[slot, p], sem_ki.at[slot])

    def start_ki(step, slot):
        for p in range(IDX_PAGES):
            ki_copy(step, slot, p).start()

    def wait_ki(step, slot):
        for p in range(IDX_PAGES):
            ki_copy(step, slot, p).wait()

    start_ki(0, 0)

    def seq_operands(s):
        r0 = s * ds if isinstance(s, int) else pl.multiple_of(s * ds, ds)
        qi = qi_ref[pl.ds(r0, ds), :]
        kw = kwn_ref[pl.ds(r0, ds), :]
        qh = jnp.concatenate([qi[:, h * IDX_DIM:(h + 1) * IDX_DIM] for h in range(IDX_HEADS)], axis=0)
        w8 = kw[:, IDX_DIM:IDX_DIM + IDX_HEADS] * (IDX_HEADS ** -0.5)
        wcol = jnp.concatenate([w8[:, h:h + 1] for h in range(IDX_HEADS)], axis=0)
        return r0, qh.astype(BF16), wcol, kw[:, 0:IDX_DIM]

    def idx_scores(wcol, d):
        r = wcol * jnp.maximum(d * (IDX_DIM ** -0.5), 0.0)
        acc = r[0:ds]
        for h in range(1, IDX_HEADS):
            acc = acc + r[h * ds:(h + 1) * ds]
        return acc

    def idx_step(step, carry):
        slot = step % 2

        @pl.when(step + 1 < n_steps)
        def _():
            start_ki(step + 1, 1 - slot)

        wait_ki(step, slot)
        r0, qh, wcol, _ = seq_operands(step // n_ch)
        kit = jnp.concatenate([kibuf[slot, p] for p in range(IDX_PAGES)], axis=1).astype(BF16)
        cols = pl.ds(pl.multiple_of((step % n_ch) * width, width), width)
        sc_ref[pl.ds(r0, ds), cols] = idx_scores(wcol, _dot(qh, kit))
        return carry

    lax.fori_loop(0, n_steps, idx_step, 0)
    for s in range(sel_seqs):
        r0, qh, wcol, ki_new = seq_operands(s)
        ki_pad = jnp.concatenate([ki_new, jnp.zeros((LANES - ds, IDX_DIM), F32)], axis=0).astype(BF16)
        sc_ref[r0:r0 + ds, past:ltot] = idx_scores(wcol, _dot_nt(qh, ki_pad))

    qpos = past + lax.broadcasted_iota(I32, (rows, 1), 0) % ds
    kpos = lax.broadcasted_iota(I32, (1, ltot), 1)
    key_ref[...] = _sortable_key(sc_ref[...], kpos <= qpos)
    sel = _topk_mask(key_ref, k16_ref, ltot, kpos, topk, idx_bits, axis=1)
    mask_ref[...] = jnp.where(sel, 1.0, 0.0)


def _sample_attend_body(pt_ref, q_ref, kn_ref, vn_ref, mask_ref, ck_hbm, cv_hbm, o_ref, kbuf, vbuf, sem_k, sem_v,
                        *, layer, n_pages, ds):
    b = pl.program_id(0)
    past = n_pages * PAGE
    ltot = past + LANES
    n_ch = n_pages // KV_PAGES
    rows_h = N_HEADS * ds

    def kv_copy(seq, c, slot, p):
        page = pt_ref[seq, c * KV_PAGES + p]
        return (pltpu.make_async_copy(ck_hbm.at[layer, page], kbuf.at[slot, p], sem_k.at[slot]),
                pltpu.make_async_copy(cv_hbm.at[layer, page], vbuf.at[slot, p], sem_v.at[slot]))

    def start_kv(seq, c, slot):
        for p in range(KV_PAGES):
            ck, cv = kv_copy(seq, c, slot, p)
            ck.start()
            cv.start()

    def wait_kv(seq, c, slot):
        for p in range(KV_PAGES):
            ck, cv = kv_copy(seq, c, slot, p)
            ck.wait()
            cv.wait()

    @pl.when(b == 0)
    def _():
        start_kv(0, 0, 0)

    q8 = jnp.concatenate([q_ref[...] * (HEAD_DIM ** -0.5)] * N_HEADS, axis=0)
    rhead = lax.broadcasted_iota(I32, (rows_h, ATT_W), 0) // ds
    chead = lax.broadcasted_iota(I32, (rows_h, ATT_W), 1) // HEAD_DIM
    qbd = jnp.where(rhead == chead, q8, 0.0).astype(BF16)

    def attend(s, msk, pv_fn, state):
        m_old, l_old, acc = state
        on = jnp.concatenate([msk] * N_HEADS, axis=0) > 0.5
        s = jnp.where(on, s, NEG_INF)
        m_new = jnp.maximum(m_old, jnp.max(s, axis=1, keepdims=True))
        alpha = jnp.exp(m_old - m_new)
        p = jnp.where(on, jnp.exp(s - m_new), 0.0)
        l_new = alpha * l_old + jnp.sum(p, axis=1, keepdims=True)
        return m_new, l_new, alpha * acc + pv_fn(p.astype(BF16))

    def kv_step(c, state):
        slot = c % 2

        @pl.when(c + 1 < n_ch)
        def _():
            start_kv(b, c + 1, 1 - slot)

        @pl.when((c + 1 == n_ch) & (b + 1 < pl.num_programs(0)))
        def _():
            start_kv(b + 1, 0, 1 - slot)

        wait_kv(b, c, slot)
        width = KV_PAGES * PAGE
        s = jnp.concatenate(
            [_dot(qbd, kbuf[slot, p].reshape(ATT_W, PAGE).astype(BF16)) for p in range(KV_PAGES)], axis=1)

        def pv_fn(p):
            acc = _dot_nt(p[:, 0:PAGE], vbuf[slot, 0].reshape(ATT_W, PAGE).astype(BF16))
            for j in range(1, KV_PAGES):
                acc = acc + _dot_nt(p[:, j * PAGE:(j + 1) * PAGE],
                                    vbuf[slot, j].reshape(ATT_W, PAGE).astype(BF16))
            return acc

        msk = mask_ref[:, pl.ds(pl.multiple_of(c * width, width), width)]
        return attend(s, msk, pv_fn, state)

    state = (jnp.full((rows_h, 1), NEG_INF, F32), jnp.zeros((rows_h, 1), F32),
             jnp.zeros((rows_h, ATT_W), F32))
    state = lax.fori_loop(0, n_ch, kv_step, state)
    pad = jnp.zeros((LANES - ds, ATT_W), F32)
    kn = jnp.concatenate([kn_ref[...], pad], axis=0).astype(BF16)
    vn = jnp.concatenate([vn_ref[...], pad], axis=0).astype(BF16)
    _, l_fin, acc = attend(_dot_nt(qbd, kn), mask_ref[:, past:ltot], lambda p: _dot(p, vn), state)
    out = acc / l_fin
    for h in range(N_HEADS):
        hs = slice(h * HEAD_DIM, (h + 1) * HEAD_DIM)
        o_ref[:, hs] = out[h * ds:(h + 1) * ds, hs]


def attn_sample(z, page_table, cache_kidx, cache_k, cache_v, *, layer, row0, n_seq, ds, topk):
    n_pages = page_table.shape[1]
    past = n_pages * PAGE
    ltot = past + LANES
    idx_bits = max(1, (ltot - 1).bit_length())
    cq, ck, cv, cqi = Z_Q // ATT_W, Z_K // ATT_W, Z_V // ATT_W, Z_QI // ATT_W
    ckw = Z_KI // LANES
    sel_seqs = min(SEL_SEQS, n_seq)
    rows = sel_seqs * ds
    rbs = row0 // rows
    assert n_seq % sel_seqs == 0 and row0 % rows == 0 and (n_pages // KV_PAGES) % 2 == 0
    sel_body = functools.partial(_sample_select_body, layer=layer, n_pages=n_pages, ds=ds, topk=topk,
                                 idx_bits=idx_bits)
    mask = pl.pallas_call(
        sel_body,
        grid_spec=pltpu.PrefetchScalarGridSpec(
            num_scalar_prefetch=1,
            grid=(n_seq // sel_seqs,),
            in_specs=[
                pl.BlockSpec((rows, ATT_W), lambda g, pt: (rbs + g, cqi)),
                pl.BlockSpec((rows, LANES), lambda g, pt: (rbs + g, ckw)),
                pl.BlockSpec(memory_space=pl.ANY),
            ],
            out_specs=pl.BlockSpec((rows, ltot), lambda g, pt: (g, 0)),
            scratch_shapes=[
                pltpu.VMEM((2, IDX_PAGES, IDX_DIM, PAGE), F32),
                pltpu.VMEM((rows, ltot), F32),
                pltpu.VMEM((rows, ltot), I32),
                pltpu.VMEM((rows, ltot), jnp.int16),
                pltpu.SemaphoreType.DMA((2,)),
            ],
        ),
        out_shape=jax.ShapeDtypeStruct((n_seq * ds, ltot), F32),
        compiler_params=_params(("arbitrary",)),
        name="sample_select",
    )(page_table, z, z, cache_kidx)

    rb = row0 // ds
    att_body = functools.partial(_sample_attend_body, layer=layer, n_pages=n_pages, ds=ds)
    return pl.pallas_call(
        att_body,
        grid_spec=pltpu.PrefetchScalarGridSpec(
            num_scalar_prefetch=1,
            grid=(n_seq,),
            in_specs=[
                pl.BlockSpec((ds, ATT_W), lambda b, pt: (rb + b, cq)),
                pl.BlockSpec((ds, ATT_W), lambda b, pt: (rb + b, ck)),
                pl.BlockSpec((ds, ATT_W), lambda b, pt: (rb + b, cv)),
                pl.BlockSpec((ds, ltot), lambda b, pt: (b, 0)),
                pl.BlockSpec(memory_space=pl.ANY),
                pl.BlockSpec(memory_space=pl.ANY),
            ],
            out_specs=pl.BlockSpec((ds, ATT_W), lambda b, pt: (b, 0)),
            scratch_shapes=[
                pltpu.VMEM((2, KV_PAGES, N_HEADS, HEAD_DIM, PAGE), F32),
                pltpu.VMEM((2, KV_PAGES, N_HEADS, HEAD_DIM, PAGE), F32),
                pltpu.SemaphoreType.DMA((2,)),
                pltpu.SemaphoreType.DMA((2,)),
            ],
        ),
        out_shape=jax.ShapeDtypeStruct((n_seq * ds, ATT_W), F32),
        compiler_params=_params(("arbitrary",)),
        name="sample_attend",
    )(page_table, z, z, z, mask, cache_k, cache_v)


def _hgrn_chunk(zq, zf, vv, zg, valid, lbf, gn, st_ref):
    c = zq.shape[0]
    levels = c.bit_length() - 1
    sig = _sigmoid(zf)
    f = lbf + (1.0 - lbf) * sig
    logf = jnp.where(valid, jnp.log(f), 0.0)
    kk = jnp.where(valid, (1.0 - lbf) * _sigmoid(-zf), 0.0)
    qq = zq * _sigmoid(zq)
    vb = jnp.where(valid, vv, 0.0).astype(BF16)

    tq = lax.broadcasted_iota(I32, (c, 1), 0)
    ts = lax.broadcasted_iota(I32, (1, c), 1)
    tril = (ts <= tq).astype(BF16)
    hi = logf.astype(BF16)
    r1 = logf - hi.astype(F32)
    mid = r1.astype(BF16)
    lo = (r1 - mid.astype(F32)).astype(BF16)
    bcum = _dot(tril, hi) + _dot(tril, mid) + _dot(tril, lo)

    x = tq ^ ts
    heads = [slice(h * HG_DK, (h + 1) * HG_DK) for h in range(HG_HEADS)]
    qb0, kb0 = qq.astype(BF16), kk.astype(BF16)
    amat = [jnp.where(x == 0, _dot_nt(qb0[:, hs], kb0[:, hs]), 0.0) for hs in heads]
    gprev = jnp.where(tq == 0, 0.0, pltpu.roll(bcum, 1, 0))
    hend = bcum
    for lev in range(levels):
        m = 1 << lev
        qs = (qq * jnp.exp(bcum - gprev)).astype(BF16)
        ks = (kk * jnp.exp(hend - bcum)).astype(BF16)
        sib = ((x >> lev) == 1) & (ts < tq)
        for h, hs in enumerate(heads):
            amat[h] = amat[h] + jnp.where(sib, _dot_nt(qs[:, hs], ks[:, hs]), 0.0)
        upper = ((tq >> lev) & 1) == 1
        gprev = jnp.where(upper, pltpu.roll(gprev, m, 0), gprev)
        hend = jnp.where(upper, hend, pltpu.roll(hend, c - m, 0))
    qs = (qq * jnp.exp(bcum)).astype(BF16)
    ks = (kk * jnp.exp(hend - bcum)).astype(BF16)
    decay = jnp.exp(hend[0:1, :])
    outs = []
    for h, hs in enumerate(heads):
        st = st_ref[h]
        o = _dot_nt(qs[:, hs], st.astype(BF16)) + _dot(amat[h].astype(BF16), vb[:, hs])
        st_ref[h] = st * decay[:, hs] + _dot_tn(vb[:, hs], ks[:, hs])
        ms = jnp.mean(o * o, axis=-1, keepdims=True)
        g = zg[:, hs]
        outs.append(o * lax.rsqrt(ms + RMS_EPS) * gn * (g * _sigmoid(g)))
    return jnp.concatenate(outs, axis=1)


def _lower_bound(lbraw_ref, layer):
    r = lbraw_ref[...]
    e = jnp.exp(r - jnp.max(r, axis=0, keepdims=True))
    sm = e / jnp.sum(e, axis=0, keepdims=True)
    cs = sm[0:1]
    for l in range(1, layer + 1):
        cs = cs + sm[l:l + 1]
    return jnp.maximum(cs - sm[0:1], LB_FLOOR)


def _hgrn_body(hq_ref, hf_ref, hv_ref, hg_ref, lbraw_ref, gn_ref, s0_ref, y_ref, sout_ref, st_ref, *,
               layer, rows_in, n_valid):
    c = HG_CHUNK
    lbf = _lower_bound(lbraw_ref, layer)
    gn = gn_ref[...]
    for h in range(HG_HEADS):
        st_ref[h] = s0_ref[0, h].T
    tq = lax.broadcasted_iota(I32, (c, 1), 0)
    if rows_in < c:
        pad = jnp.zeros((c - rows_in, HG_W), F32)

        def padded(ref):
            return jnp.concatenate([ref[...], pad], axis=0)

        y = _hgrn_chunk(padded(hq_ref), padded(hf_ref), padded(hv_ref), padded(hg_ref), tq < n_valid, lbf,
                        gn, st_ref)
        y_ref[...] = y[0:rows_in]
    else:
        def step(ci, carry):
            rows = pl.ds(pl.multiple_of(ci * c, c), c)
            y_ref[rows, :] = _hgrn_chunk(hq_ref[rows, :], hf_ref[rows, :], hv_ref[rows, :], hg_ref[rows, :],
                                         (ci * c + tq) < n_valid, lbf, gn, st_ref)
            return carry

        lax.fori_loop(0, rows_in // c, step, 0)
    for h in range(HG_HEADS):
        sout_ref[0, h] = st_ref[h].T


def hgrn(z, lb_raw, hg_norm, s0, *, layer, row0, n_seq, rows_in, n_valid):
    rb = row0 // rows_in
    depth = lb_raw.shape[0]
    body = functools.partial(_hgrn_body, layer=layer, rows_in=rows_in, n_valid=n_valid)
    cols = [Z_HQ // HG_W, Z_HF // HG_W, Z_HV // HG_W, Z_HG // HG_W]
    zspec = [pl.BlockSpec((rows_in, HG_W), functools.partial(lambda b, cc: (rb + b, cc), cc=cc)) for cc in cols]
    return pl.pallas_call(
        body,
        grid=(n_seq,),
        in_specs=zspec + [
            pl.BlockSpec((depth, HG_W), lambda b: (0, 0)),
            pl.BlockSpec((1, HG_DV), lambda b: (0, 0)),
            pl.BlockSpec((1, HG_HEADS, HG_DK, HG_DV), lambda b: (b, 0, 0, 0)),
        ],
        out_specs=[
            pl.BlockSpec((rows_in, HG_W), lambda b: (b, 0)),
            pl.BlockSpec((1, HG_HEADS, HG_DK, HG_DV), lambda b: (b, 0, 0, 0)),
        ],
        out_shape=[
            jax.ShapeDtypeStruct((n_seq * rows_in, HG_W), F32),
            jax.ShapeDtypeStruct((n_seq, HG_HEADS, HG_DK, HG_DV), F32),
        ],
        scratch_shapes=[pltpu.VMEM((HG_HEADS, HG_DV, HG_DK), F32)],
        compiler_params=_params(("parallel",)),
        name="hgrn",
    )(z, z, z, z, lb_raw, hg_norm, s0)


def _permute_w_in(w):
    o = ORIG_OFFS
    parts = [w[:, o["q"]:o["q"] + 2048],
             w[:, o["hq"]:o["hq"] + 2048],
             w[:, o["ga"]:o["ga"] + 2048],
             w[:, o["ki"]:o["ki"] + IDX_DIM],
             w[:, o["wi"]:o["wi"] + IDX_HEADS],
             jnp.zeros((w.shape[0], Z_W - Z_WI - IDX_HEADS), w.dtype)]
    return jnp.concatenate(parts, axis=1).astype(BF16)


ROW_TILE = 768


def kernel(x_prompt, x_sample, cache_k, cache_v, cache_kidx, state_hgrn, page_table, meta_tokens, norm_gains,
           ffn1_w_gu, ffn1_w_down, w_in, lb_raw, hg_norm, w_branch_attn, w_branch_hgrn, w_out, ffn2_w_gu,
           ffn2_w_down):
    return _forward(x_prompt, x_sample, cache_k, cache_v, cache_kidx, state_hgrn, page_table, meta_tokens,
                    norm_gains, ffn1_w_gu, ffn1_w_down, w_in, lb_raw, hg_norm, w_branch_attn, w_branch_hgrn,
                    w_out, ffn2_w_gu, ffn2_w_down, tm=ROW_TILE)


def _forward(x_prompt, x_sample, cache_k, cache_v, cache_kidx, state_hgrn, page_table, meta_tokens, norm_gains,
             ffn1_w_gu, ffn1_w_down, w_in, lb_raw, hg_norm, w_branch_attn, w_branch_hgrn, w_out, ffn2_w_gu,
             ffn2_w_down, *, tm):
    n_seq, seq, d = x_prompt.shape
    db, ds, _ = x_sample.shape
    depth = w_in.shape[0]
    t_real = seq + N_META
    t_pad = -(-t_real // QB) * QB
    rows_p = n_seq * t_pad
    rows = rows_p + db * ds
    assert rows % tm == 0
    topk_p = min(TOPK, t_real // 4)
    topk_s = min(TOPK, (page_table.shape[1] * PAGE + ds) // 4)

    meta = jnp.broadcast_to(meta_tokens[None].astype(F32), (n_seq, N_META, d))
    xp = jnp.concatenate([meta, x_prompt, jnp.zeros((n_seq, t_pad - t_real, d), F32)], axis=1)
    x = jnp.concatenate([xp.reshape(rows_p, d), x_sample.reshape(db * ds, d)], axis=0)

    ck = cache_k.transpose(0, 1, 3, 4, 2)
    cv = cache_v.transpose(0, 1, 3, 4, 2)
    cki = cache_kidx.transpose(0, 1, 3, 2)
    zero_state = jnp.zeros((n_seq, HG_HEADS, HG_DK, HG_DV), F32)
    hgn = hg_norm.reshape(depth, 1, HG_DV)

    outs = {k: [] for k in ("kp", "vp", "kip", "sp", "ks", "vs", "kis", "ss")}
    for l in range(depth):
        g = norm_gains[l].reshape(6, 1, d)
        x = ffn(x, g[0], g[1], ffn1_w_gu[l].astype(BF16), ffn1_w_down[l].astype(BF16), tm=tm, tf=256)
        z = inproj(x, g[2], _permute_w_in(w_in[l]), tm=tm, tn=896)
        ya_p = attn_prompt(z, n_seq=n_seq, t_pad=t_pad, topk=topk_p)
        ya_s = attn_sample(z, page_table, cki, ck, cv, layer=l, row0=rows_p, n_seq=db, ds=ds,
                           topk=topk_s)
        yh_p, st_p = hgrn(z, lb_raw, hgn[l], zero_state, layer=l, row0=0, n_seq=n_seq, rows_in=t_pad,
                          n_valid=t_real)
        yh_s, st_s = hgrn(z, lb_raw, hgn[l], state_hgrn[l], layer=l, row0=rows_p, n_seq=db, rows_in=ds,
                          n_valid=ds)
        ya = jnp.concatenate([ya_p, ya_s], axis=0)
        yh = jnp.concatenate([yh_p, yh_s], axis=0)
        x = combine(x, ya, yh, z, w_branch_attn[l].astype(BF16), w_branch_hgrn[l].astype(BF16),
                    w_out[l].astype(BF16), g[3], tm=tm)
        x = ffn(x, g[4], g[5], ffn2_w_gu[l].astype(BF16), ffn2_w_down[l].astype(BF16), tm=tm, tf=256)

        def cols_p(c0, w):
            return lax.slice(z, (0, c0), (rows_p, c0 + w)).reshape(n_seq, t_pad, w)[:, :t_real]

        def cols_s(c0, w):
            return lax.slice(z, (rows_p, c0), (rows, c0 + w)).reshape(db, ds, w)

        outs["kp"].append(cols_p(Z_K, ATT_W).reshape(n_seq, t_real, N_HEADS, HEAD_DIM))
        outs["vp"].append(cols_p(Z_V, ATT_W).reshape(n_seq, t_real, N_HEADS, HEAD_DIM))
        outs["kip"].append(cols_p(Z_KI, IDX_DIM))
        outs["sp"].append(st_p)
        outs["ks"].append(cols_s(Z_K, ATT_W).reshape(db, ds, N_HEADS, HEAD_DIM))
        outs["vs"].append(cols_s(Z_V, ATT_W).reshape(db, ds, N_HEADS, HEAD_DIM))
        outs["kis"].append(cols_s(Z_KI, IDX_DIM))
        outs["ss"].append(st_s)

    y_prompt = x[:rows_p].reshape(n_seq, t_pad, d)[:, N_META:t_real]
    y_sample = x[rows_p:].reshape(db, ds, d)
    st = lambda k: jnp.stack(outs[k])
    return (y_prompt, y_sample, st("kp"), st("vp"), st("kip"), st("sp"), st("ks"), st("vs"), st("kis"),
            st("ss"))
```

```python
import functools

import jax
import jax.numpy as jnp
from jax import lax
from jax.experimental import pallas as pl
from jax.experimental.pallas import tpu as pltpu

F32, BF16, I32 = jnp.float32, jnp.bfloat16, jnp.int32

D_MODEL = 1024
N_META = 16
N_HEADS = 8
HEAD_DIM = 64
ATT_W = N_HEADS * HEAD_DIM
IDX_HEADS = 8
IDX_DIM = 64
TOPK = 256
HG_HEADS = 4
HG_DK = 128
HG_DV = 128
HG_W = HG_HEADS * HG_DK
D_FF = 2816
PAGE = 128
RMS_EPS = 1e-6
NEG_INF = -1e30
LB_FLOOR = 1e-30
INT_MIN = -(2 ** 31)

LANES = 128
SUBLANES = 8
VMEM_LIMIT = 56 * 1024 * 1024

Z_Q, Z_K, Z_V, Z_QI = 0, 512, 1024, 1536
Z_HQ, Z_HF, Z_HV, Z_HG = 2048, 2560, 3072, 3584
Z_GA, Z_GH = 4096, 5120
Z_KI, Z_WI = 6144, 6208
Z_W = 6272
ORIG_OFFS = dict(q=0, k=512, v=1024, qi=1536, wi=2048, ki=2056, hq=2120, hf=2632, hv=3144,
                 hg=3656, ga=4168, gh=5192)

QB = 128
KV_STEP = 256
HG_CHUNK = 128


def _dot(a, b):
    return jnp.dot(a, b, preferred_element_type=F32)


def _dot_nt(a, b):
    return lax.dot_general(a, b, (((1,), (1,)), ((), ())), preferred_element_type=F32)


def _dot_tn(a, b):
    return lax.dot_general(a, b, (((0,), (0,)), ((), ())), preferred_element_type=F32)


def _rms(x, g):
    ms = jnp.mean(x * x, axis=-1, keepdims=True)
    return x * lax.rsqrt(ms + RMS_EPS) * g


def _sigmoid(x):
    return 1.0 / (1.0 + jnp.exp(-x))


def _params(sem):
    return pltpu.CompilerParams(dimension_semantics=sem, vmem_limit_bytes=VMEM_LIMIT)


def _ffn_body(x_ref, gpre_ref, gpost_ref, wg_ref, wu_ref, wd_ref, o_ref, h_ref, acc_ref):
    j = pl.program_id(1)

    @pl.when(j == 0)
    def _():
        h_ref[...] = _rms(x_ref[...], gpre_ref[...]).astype(BF16)
        acc_ref[...] = jnp.zeros_like(acc_ref)

    h = h_ref[...]
    g = _dot(h, wg_ref[...])
    u = _dot(h, wu_ref[...])
    a = (g * _sigmoid(g) * u).astype(BF16)
    acc_ref[...] += _dot(a, wd_ref[...])

    @pl.when(j == pl.num_programs(1) - 1)
    def _():
        o_ref[...] = x_ref[...] + 0.5 * _rms(acc_ref[...], gpost_ref[...])


def ffn(x, gpre, gpost, w_gu, w_dn, *, tm, tf):
    rows, d = x.shape
    ff = w_dn.shape[0]
    nj = ff // tf
    return pl.pallas_call(
        _ffn_body,
        grid=(rows // tm, nj),
        in_specs=[
            pl.BlockSpec((tm, d), lambda i, j: (i, 0)),
            pl.BlockSpec((1, d), lambda i, j: (0, 0)),
            pl.BlockSpec((1, d), lambda i, j: (0, 0)),
            pl.BlockSpec((d, tf), lambda i, j: (0, j)),
            pl.BlockSpec((d, tf), lambda i, j: (0, j + nj)),
            pl.BlockSpec((tf, d), lambda i, j: (j, 0)),
        ],
        out_specs=pl.BlockSpec((tm, d), lambda i, j: (i, 0)),
        out_shape=jax.ShapeDtypeStruct((rows, d), F32),
        scratch_shapes=[pltpu.VMEM((tm, d), BF16), pltpu.VMEM((tm, d), F32)],
        compiler_params=_params(("parallel", "arbitrary")),
        name="ffn",
    )(x, gpre, gpost, w_gu, w_gu, w_dn)


def _inproj_body(x_ref, g_ref, w_ref, o_ref, h_ref):
    @pl.when(pl.program_id(1) == 0)
    def _():
        h_ref[...] = _rms(x_ref[...], g_ref[...]).astype(BF16)

    o_ref[...] = _dot(h_ref[...], w_ref[...])


def inproj(x, g, w, *, tm, tn):
    rows, d = x.shape
    n = w.shape[1]
    return pl.pallas_call(
        _inproj_body,
        grid=(rows // tm, n // tn),
        in_specs=[
            pl.BlockSpec((tm, d), lambda i, j: (i, 0)),
            pl.BlockSpec((1, d), lambda i, j: (0, 0)),
            pl.BlockSpec((d, tn), lambda i, j: (0, j)),
        ],
        out_specs=pl.BlockSpec((tm, tn), lambda i, j: (i, j)),
        out_shape=jax.ShapeDtypeStruct((rows, n), F32),
        scratch_shapes=[pltpu.VMEM((tm, d), BF16)],
        compiler_params=_params(("parallel", "arbitrary")),
        name="inproj",
    )(x, g, w)


def _combine_body(x_ref, ya_ref, yh_ref, ga_ref, gh_ref, wpa_ref, wph_ref, wo_ref, g_ref, o_ref):
    a = _dot(ya_ref[...].astype(BF16), wpa_ref[...])
    hh = _dot(yh_ref[...].astype(BF16), wph_ref[...])
    m = _sigmoid(ga_ref[...]) * a + _sigmoid(gh_ref[...]) * hh
    r = _dot(m.astype(BF16), wo_ref[...])
    o_ref[...] = x_ref[...] + _rms(r, g_ref[...])


def combine(x, ya, yh, z, w_pa, w_ph, w_o, g, *, tm):
    rows, d = x.shape
    return pl.pallas_call(
        _combine_body,
        grid=(rows // tm,),
        in_specs=[
            pl.BlockSpec((tm, d), lambda i: (i, 0)),
            pl.BlockSpec((tm, ATT_W), lambda i: (i, 0)),
            pl.BlockSpec((tm, HG_W), lambda i: (i, 0)),
            pl.BlockSpec((tm, d), lambda i: (i, Z_GA // D_MODEL)),
            pl.BlockSpec((tm, d), lambda i: (i, Z_GH // D_MODEL)),
            pl.BlockSpec((ATT_W, d), lambda i: (0, 0)),
            pl.BlockSpec((HG_W, d), lambda i: (0, 0)),
            pl.BlockSpec((d, d), lambda i: (0, 0)),
            pl.BlockSpec((1, d), lambda i: (0, 0)),
        ],
        out_specs=pl.BlockSpec((tm, d), lambda i: (i, 0)),
        out_shape=jax.ShapeDtypeStruct((rows, d), F32),
        compiler_params=_params(("parallel",)),
        name="combine",
    )(x, ya, yh, z, z, w_pa, w_ph, w_o, g)


def _sortable_key(score, adm):
    s = jnp.where(score == 0.0, 0.0, score)
    b = lax.bitcast_convert_type(s, I32)
    key = b ^ ((b >> 31) & 0x7FFFFFFF)
    return jnp.where(adm, key, INT_MIN)


HALF_BIAS = 32768
PACK16 = 16
N_ACC = 4


def _topk_mask(key_ref, k16_ref, n, kpos, topk, idx_bits, axis):
    kf = float(topk)
    view = (lambda r: r[:n, :]) if axis == 0 else (lambda r: r[:, :n])
    tile = PACK16 if axis == 0 else LANES

    def tile_of(j):
        return k16_ref[j * tile:(j + 1) * tile, :] if axis == 0 else k16_ref[:, j * tile:(j + 1) * tile]

    def count(pred):
        return jnp.sum(jnp.where(pred, 1.0, 0.0), axis=axis, keepdims=True)

    def count16(cand16):
        accs = [None] * N_ACC
        for j in range(n // tile):
            one = jnp.where(tile_of(j) >= cand16, jnp.int16(1), jnp.int16(0))
            accs[j % N_ACC] = one if accs[j % N_ACC] is None else accs[j % N_ACC] + one
        tot = accs[0]
        for a in accs[1:]:
            if a is not None:
                tot = tot + a
        return jnp.sum(tot.astype(F32), axis=axis, keepdims=True)

    def search16(need):
        def step(it, thr_u):
            cand_u = thr_u | jnp.left_shift(jnp.int32(1), 15 - it)
            cnt = count16((cand_u - HALF_BIAS).astype(jnp.int16))
            return jnp.where(cnt >= need, cand_u, thr_u)

        shape = (1, key_ref.shape[1]) if axis == 0 else (key_ref.shape[0], 1)
        return lax.fori_loop(0, 16, step, jnp.zeros(shape, I32))

    key = view(key_ref)
    hi = key >> 16
    if axis == 0:
        k16_ref[:n, :] = hi.astype(jnp.int16)
    else:
        k16_ref[:, :n] = hi.astype(jnp.int16)
    thr_hi = search16(kf) - HALF_BIAS
    need_lo = kf - count(hi > thr_hi)
    lo = jnp.where(hi == thr_hi, (key & 0xFFFF) - HALF_BIAS, -HALF_BIAS).astype(jnp.int16)
    if axis == 0:
        k16_ref[:n, :] = lo
    else:
        k16_ref[:, :n] = lo
    thr = (thr_hi << 16) | search16(need_lo)

    gt = key > thr
    eq = key == thr
    need = kf - count(gt)
    tie = ((count(eq) > need) & (thr != INT_MIN)).astype(F32)

    def index_search():
        def step(it, bound):
            cand = bound | jnp.left_shift(jnp.int32(1), idx_bits - 1 - it)
            c = count((view(key_ref) == thr) & (kpos < cand))
            return jnp.where(c <= need, cand, bound)

        return lax.fori_loop(0, idx_bits, step, jnp.zeros(thr.shape, I32))

    bound = lax.cond(jnp.max(tie) > 0.5, index_search, lambda: jnp.full(thr.shape, 1 << idx_bits, I32))
    return (gt | (eq & (kpos < bound))) & (key != INT_MIN)


def _attn_prompt_variant(lk, q_ref, qi_ref, kwq_ref, kb_ref, vb_ref, ki2_ref, o_ref, key_ref, k16_ref, bias_ref,
                         topk, idx_bits):
    i = pl.program_id(1)
    low = lax.broadcasted_iota(I32, (QB, LANES), 1) < HEAD_DIM

    def split_pair(tile):
        zero = jnp.zeros_like(tile)
        return jnp.concatenate([jnp.where(low, tile, zero), jnp.where(low, zero, tile)], axis=0)

    qpos = i * QB + lax.broadcasted_iota(I32, (1, QB), 1)
    kpos = lax.broadcasted_iota(I32, (lk, 1), 0)
    w8t = kwq_ref[...].T[IDX_DIM:IDX_DIM + IDX_HEADS, :] * ((IDX_HEADS ** -0.5) * (IDX_DIM ** -0.5))
    ki2 = ki2_ref[0:lk, :]
    qi = qi_ref[...].astype(BF16)
    sc = jnp.zeros((lk, QB), F32)
    for j in range(IDX_HEADS // 2):
        d = _dot_nt(ki2, split_pair(qi[:, j * LANES:(j + 1) * LANES]))
        sc = sc + w8t[2 * j:2 * j + 1, :] * jnp.maximum(d[:, :QB], 0.0)
        sc = sc + w8t[2 * j + 1:2 * j + 2, :] * jnp.maximum(d[:, QB:], 0.0)
    key_ref[:lk, :] = _sortable_key(sc, kpos <= qpos)
    sel = _topk_mask(key_ref, k16_ref, lk, kpos, topk, idx_bits, axis=0)
    bias_ref[:, :lk] = jnp.where(sel, 0.0, NEG_INF).T

    q = (q_ref[...] * (HEAD_DIM ** -0.5)).astype(BF16)
    bias = bias_ref[:, :lk]
    for j in range(N_HEADS // 2):
        ts = slice(j * LANES, (j + 1) * LANES)
        s2 = _dot_nt(split_pair(q[:, ts]), kb_ref[0:lk, ts])
        ps, ls = [], []
        for half in range(2):
            s = s2[half * QB:(half + 1) * QB] + bias
            m = jnp.max(s, axis=1, keepdims=True)
            p = jnp.exp(s - m)
            ls.append(jnp.sum(p, axis=1, keepdims=True))
            ps.append(p.astype(BF16))
        pv = _dot(jnp.concatenate(ps, axis=0), vb_ref[0:lk, ts])
        o_ref[:, ts] = jnp.where(low, pv[:QB] / ls[0], pv[QB:] / ls[1])


def _attn_prompt_body(q_ref, qi_ref, kwq_ref, k_ref, v_ref, kwk_ref, y_init_hbm, o_ref, kb_ref, vb_ref, ki2_ref, key_ref,
                      k16_ref, bias_ref, *, t_pad, topk, idx_bits):
    i = pl.program_id(1)

    @pl.when(i == 0)
    def _():
        kb_ref[...] = k_ref[...].astype(BF16)
        vb_ref[...] = v_ref[...].astype(BF16)
        kw = kwk_ref[...]
        first = lax.broadcasted_iota(I32, kw.shape, 1) < IDX_DIM
        ki2_ref[...] = jnp.where(first, kw, pltpu.roll(kw, IDX_DIM, 1)).astype(BF16)

    nqb = t_pad // QB
    per = KV_STEP // QB
    lo = 0
    while lo < nqb:
        hi = min(lo + per, nqb)
        lk = hi * QB

        @pl.when((i >= lo) & (i < hi))
        def _(lk=lk):
            _attn_prompt_variant(lk, q_ref, qi_ref, kwq_ref, kb_ref, vb_ref, ki2_ref, o_ref, key_ref, k16_ref,
                                 bias_ref, topk, idx_bits)

        lo = hi


def attn_prompt(z, y_init, *, n_seq, t_pad, topk):
    nqb = t_pad // QB
    idx_bits = max(1, (t_pad - 1).bit_length())
    cq, ck, cv, cqi = Z_Q // ATT_W, Z_K // ATT_W, Z_V // ATT_W, Z_QI // ATT_W
    ckw = Z_KI // LANES
    body = functools.partial(_attn_prompt_body, t_pad=t_pad, topk=topk, idx_bits=idx_bits)
    return pl.pallas_call(
        body,
        grid=(n_seq, nqb),
        in_specs=[
            pl.BlockSpec((QB, ATT_W), lambda b, i: (b * nqb + i, cq)),
            pl.BlockSpec((QB, ATT_W), lambda b, i: (b * nqb + i, cqi)),
            pl.BlockSpec((QB, LANES), lambda b, i: (b * nqb + i, ckw)),
            pl.BlockSpec((t_pad, ATT_W), lambda b, i: (b, ck)),
            pl.BlockSpec((t_pad, ATT_W), lambda b, i: (b, cv)),
            pl.BlockSpec((t_pad, LANES), lambda b, i: (b, ckw)),
            pl.BlockSpec(memory_space=pl.ANY),
        ],
        out_specs=pl.BlockSpec((QB, ATT_W), lambda b, i: (b * nqb + i, 0)),
        out_shape=jax.ShapeDtypeStruct(y_init.shape, F32),
        input_output_aliases={6: 0},
        scratch_shapes=[pltpu.VMEM((t_pad, ATT_W), BF16), pltpu.VMEM((t_pad, ATT_W), BF16),
                        pltpu.VMEM((t_pad, LANES), BF16),
                        pltpu.VMEM((t_pad, QB), I32), pltpu.VMEM((t_pad, QB), jnp.int16),
                        pltpu.VMEM((QB, t_pad), F32)],
        compiler_params=_params(("parallel", "arbitrary")),
        name="attn_prompt",
    )(z, z, z, z, z, z, y_init)


IDX_PAGES = 16
KV_PAGES = 8
KV_SLOTS = 3


SEL_SEQS = 8


def _sample_select_body(pt_ref, qi_ref, kwn_ref, ckidx_hbm, mask_ref, kibuf, sc_ref, key_ref, k16_ref, sem_ki, *,
                        layer, n_pages, ds, topk, idx_bits):
    g = pl.program_id(0)
    past = n_pages * PAGE
    ltot = past + LANES
    n_ch = n_pages // IDX_PAGES
    rows = sc_ref.shape[0]
    sel_seqs = rows // ds
    n_steps = sel_seqs * n_ch
    width = IDX_PAGES * PAGE

    def ki_copy(step, slot, p):
        seq = g * sel_seqs + step // n_ch
        page = pt_ref[seq, (step % n_ch) * IDX_PAGES + p]
        return pltpu.make_async_copy(ckidx_hbm.at[layer, page], kibuf.at[slot, p], sem_ki.at[slot])

    def start_ki(step, slot):
        for p in range(IDX_PAGES):
            ki_copy(step, slot, p).start()

    def wait_ki(step, slot):
        for p in range(IDX_PAGES):
            ki_copy(step, slot, p).wait()

    start_ki(0, 0)

    def seq_operands(s):
        r0 = s * ds if isinstance(s, int) else pl.multiple_of(s * ds, ds)
        qi = qi_ref[pl.ds(r0, ds), :]
        kw = kwn_ref[pl.ds(r0, ds), :]
        qh = jnp.concatenate([qi[:, h * IDX_DIM:(h + 1) * IDX_DIM] for h in range(IDX_HEADS)], axis=0)
        w8 = kw[:, IDX_DIM:IDX_DIM + IDX_HEADS] * (IDX_HEADS ** -0.5)
        wcol = jnp.concatenate([w8[:, h:h + 1] for h in range(IDX_HEADS)], axis=0)
        return r0, qh.astype(BF16), wcol, kw[:, 0:IDX_DIM]

    def idx_scores(wcol, d):
        r = wcol * jnp.maximum(d * (IDX_DIM ** -0.5), 0.0)
        acc = r[0:ds]
        for h in range(1, IDX_HEADS):
            acc = acc + r[h * ds:(h + 1) * ds]
        return acc

    def idx_step(step, carry):
        slot = step % 2

        @pl.when(step + 1 < n_steps)
        def _():
            start_ki(step + 1, 1 - slot)

        wait_ki(step, slot)
        r0, qh, wcol, _ = seq_operands(step // n_ch)
        kit = jnp.concatenate([kibuf[slot, p] for p in range(IDX_PAGES)], axis=1).astype(BF16)
        cols = pl.ds(pl.multiple_of((step % n_ch) * width, width), width)
        sc_ref[pl.ds(r0, ds), cols] = idx_scores(wcol, _dot(qh, kit))
        return carry

    lax.fori_loop(0, n_steps, idx_step, 0)
    for s in range(sel_seqs):
        r0, qh, wcol, ki_new = seq_operands(s)
        ki_pad = jnp.concatenate([ki_new, jnp.zeros((LANES - ds, IDX_DIM), F32)], axis=0).astype(BF16)
        sc_ref[r0:r0 + ds, past:ltot] = idx_scores(wcol, _dot_nt(qh, ki_pad))

    qpos = past + lax.broadcasted_iota(I32, (rows, 1), 0) % ds
    kpos = lax.broadcasted_iota(I32, (1, ltot), 1)
    key_ref[...] = _sortable_key(sc_ref[...], kpos <= qpos)
    sel = _topk_mask(key_ref, k16_ref, ltot, kpos, topk, idx_bits, axis=1)
    mask_ref[...] = jnp.where(sel, 1.0, 0.0)


def _sample_attend_body(pt_ref, q_ref, kn_ref, vn_ref, mask_ref, ck_hbm, cv_hbm, ya_hbm, o_ref, kbuf, vbuf, sem_k, sem_v,
                        *, layer, n_pages, ds):
    b = pl.program_id(0)
    past = n_pages * PAGE
    ltot = past + LANES
    n_ch = n_pages // KV_PAGES
    rows_h = N_HEADS * ds

    n_total = pl.num_programs(0) * n_ch

    def kv_copy(g, p):
        slot = g % KV_SLOTS
        page = pt_ref[g // n_ch, (g % n_ch) * KV_PAGES + p]
        return (pltpu.make_async_copy(ck_hbm.at[layer, page], kbuf.at[slot, p], sem_k.at[slot]),
                pltpu.make_async_copy(cv_hbm.at[layer, page], vbuf.at[slot, p], sem_v.at[slot]))

    def start_kv(g):
        for p in range(KV_PAGES):
            ck, cv = kv_copy(g, p)
            ck.start()
            cv.start()

    def wait_kv(g):
        for p in range(KV_PAGES):
            ck, cv = kv_copy(g, p)
            ck.wait()
            cv.wait()

    @pl.when(b == 0)
    def _():
        for g in range(KV_SLOTS - 1):
            start_kv(g)

    q8 = jnp.concatenate([q_ref[...] * (HEAD_DIM ** -0.5)] * N_HEADS, axis=0)
    rhead = lax.broadcasted_iota(I32, (rows_h, ATT_W), 0) // ds
    chead = lax.broadcasted_iota(I32, (rows_h, ATT_W), 1) // HEAD_DIM
    qbd = jnp.where(rhead == chead, q8, 0.0).astype(BF16)

    def attend(s, msk, pv_fn, state):
        m_old, l_old, acc = state
        on = jnp.concatenate([msk] * N_HEADS, axis=0) > 0.5
        s = jnp.where(on, s, NEG_INF)
        m_new = jnp.maximum(m_old, jnp.max(s, axis=1, keepdims=True))
        alpha = jnp.exp(m_old - m_new)
        p = jnp.where(on, jnp.exp(s - m_new), 0.0)
        l_new = alpha * l_old + jnp.sum(p, axis=1, keepdims=True)
        return m_new, l_new, alpha * acc + pv_fn(p.astype(BF16))

    def kv_step(c, state):
        g = b * n_ch + c
        slot = g % KV_SLOTS
        ahead = g + (KV_SLOTS - 1)

        @pl.when(ahead < n_total)
        def _():
            start_kv(ahead)

        wait_kv(g)
        width = KV_PAGES * PAGE
        s = jnp.concatenate(
            [_dot(qbd, kbuf[slot, p].reshape(ATT_W, PAGE).astype(BF16)) for p in range(KV_PAGES)], axis=1)

        def pv_fn(p):
            acc = _dot_nt(p[:, 0:PAGE], vbuf[slot, 0].reshape(ATT_W, PAGE).astype(BF16))
            for j in range(1, KV_PAGES):
                acc = acc + _dot_nt(p[:, j * PAGE:(j + 1) * PAGE],
                                    vbuf[slot, j].reshape(ATT_W, PAGE).astype(BF16))
            return acc

        msk = mask_ref[:, pl.ds(pl.multiple_of(c * width, width), width)]
        return attend(s, msk, pv_fn, state)

    state = (jnp.full((rows_h, 1), NEG_INF, F32), jnp.zeros((rows_h, 1), F32),
             jnp.zeros((rows_h, ATT_W), F32))
    state = lax.fori_loop(0, n_ch, kv_step, state)
    pad = jnp.zeros((LANES - ds, ATT_W), F32)
    kn = jnp.concatenate([kn_ref[...], pad], axis=0).astype(BF16)
    vn = jnp.concatenate([vn_ref[...], pad], axis=0).astype(BF16)
    _, l_fin, acc = attend(_dot_nt(qbd, kn), mask_ref[:, past:ltot], lambda p: _dot(p, vn), state)
    out = acc / l_fin
    for h in range(N_HEADS):
        hs = slice(h * HEAD_DIM, (h + 1) * HEAD_DIM)
        o_ref[:, hs] = out[h * ds:(h + 1) * ds, hs]


def attn_sample(z, page_table, cache_kidx, cache_k, cache_v, ya, *, layer, row0, n_seq, ds, topk):
    n_pages = page_table.shape[1]
    past = n_pages * PAGE
    ltot = past + LANES
    idx_bits = max(1, (ltot - 1).bit_length())
    cq, ck, cv, cqi = Z_Q // ATT_W, Z_K // ATT_W, Z_V // ATT_W, Z_QI // ATT_W
    ckw = Z_KI // LANES
    sel_seqs = min(SEL_SEQS, n_seq)
    rows = sel_seqs * ds
    rbs = row0 // rows
    assert n_seq % sel_seqs == 0 and row0 % rows == 0
    sel_body = functools.partial(_sample_select_body, layer=layer, n_pages=n_pages, ds=ds, topk=topk,
                                 idx_bits=idx_bits)
    mask = pl.pallas_call(
        sel_body,
        grid_spec=pltpu.PrefetchScalarGridSpec(
            num_scalar_prefetch=1,
            grid=(n_seq // sel_seqs,),
            in_specs=[
                pl.BlockSpec((rows, ATT_W), lambda g, pt: (rbs + g, cqi)),
                pl.BlockSpec((rows, LANES), lambda g, pt: (rbs + g, ckw)),
                pl.BlockSpec(memory_space=pl.ANY),
            ],
            out_specs=pl.BlockSpec((rows, ltot), lambda g, pt: (g, 0)),
            scratch_shapes=[
                pltpu.VMEM((2, IDX_PAGES, IDX_DIM, PAGE), F32),
                pltpu.VMEM((rows, ltot), F32),
                pltpu.VMEM((rows, ltot), I32),
                pltpu.VMEM((rows, ltot), jnp.int16),
                pltpu.SemaphoreType.DMA((2,)),
            ],
        ),
        out_shape=jax.ShapeDtypeStruct((n_seq * ds, ltot), F32),
        compiler_params=_params(("arbitrary",)),
        name="sample_select",
    )(page_table, z, z, cache_kidx)

    rb = row0 // ds
    att_body = functools.partial(_sample_attend_body, layer=layer, n_pages=n_pages, ds=ds)
    return pl.pallas_call(
        att_body,
        grid_spec=pltpu.PrefetchScalarGridSpec(
            num_scalar_prefetch=1,
            grid=(n_seq,),
            in_specs=[
                pl.BlockSpec((ds, ATT_W), lambda b, pt: (rb + b, cq)),
                pl.BlockSpec((ds, ATT_W), lambda b, pt: (rb + b, ck)),
                pl.BlockSpec((ds, ATT_W), lambda b, pt: (rb + b, cv)),
                pl.BlockSpec((ds, ltot), lambda b, pt: (b, 0)),
                pl.BlockSpec(memory_space=pl.ANY),
                pl.BlockSpec(memory_space=pl.ANY),
                pl.BlockSpec(memory_space=pl.ANY),
            ],
            out_specs=pl.BlockSpec((ds, ATT_W), lambda b, pt: (rb + b, 0)),
            scratch_shapes=[
                pltpu.VMEM((KV_SLOTS, KV_PAGES, N_HEADS, HEAD_DIM, PAGE), F32),
                pltpu.VMEM((KV_SLOTS, KV_PAGES, N_HEADS, HEAD_DIM, PAGE), F32),
                pltpu.SemaphoreType.DMA((KV_SLOTS,)),
                pltpu.SemaphoreType.DMA((KV_SLOTS,)),
            ],
        ),
        out_shape=jax.ShapeDtypeStruct(ya.shape, F32),
        input_output_aliases={7: 0},
        compiler_params=_params(("arbitrary",)),
        name="sample_attend",
    )(page_table, z, z, z, mask, cache_k, cache_v, ya)


def _hgrn_chunk(zq, zf, vv, zg, valid, lbf, gn, st_ref):
    c = zq.shape[0]
    levels = c.bit_length() - 1
    sig = _sigmoid(zf)
    f = lbf + (1.0 - lbf) * sig
    logf = jnp.where(valid, jnp.log(f), 0.0)
    kk = jnp.where(valid, (1.0 - lbf) * _sigmoid(-zf), 0.0)
    qq = zq * _sigmoid(zq)
    vb = jnp.where(valid, vv, 0.0).astype(BF16)

    tq = lax.broadcasted_iota(I32, (c, 1), 0)
    ts = lax.broadcasted_iota(I32, (1, c), 1)
    tril = (ts <= tq).astype(BF16)
    hi = logf.astype(BF16)
    r1 = logf - hi.astype(F32)
    mid = r1.astype(BF16)
    lo = (r1 - mid.astype(F32)).astype(BF16)
    bcum = _dot(tril, hi) + _dot(tril, mid) + _dot(tril, lo)

    x = tq ^ ts
    heads = [slice(h * HG_DK, (h + 1) * HG_DK) for h in range(HG_HEADS)]
    qb0, kb0 = qq.astype(BF16), kk.astype(BF16)
    amat = [jnp.where(x == 0, _dot_nt(qb0[:, hs], kb0[:, hs]), 0.0) for hs in heads]
    gprev = jnp.where(tq == 0, 0.0, pltpu.roll(bcum, 1, 0))
    hend = bcum
    for lev in range(levels):
        m = 1 << lev
        qs = (qq * jnp.exp(bcum - gprev)).astype(BF16)
        ks = (kk * jnp.exp(hend - bcum)).astype(BF16)
        sib = ((x >> lev) == 1) & (ts < tq)
        for h, hs in enumerate(heads):
            amat[h] = amat[h] + jnp.where(sib, _dot_nt(qs[:, hs], ks[:, hs]), 0.0)
        upper = ((tq >> lev) & 1) == 1
        gprev = jnp.where(upper, pltpu.roll(gprev, m, 0), gprev)
        hend = jnp.where(upper, hend, pltpu.roll(hend, c - m, 0))
    qs = (qq * jnp.exp(bcum)).astype(BF16)
    ks = (kk * jnp.exp(hend - bcum)).astype(BF16)
    decay = jnp.exp(hend[0:1, :])
    outs = []
    for h, hs in enumerate(heads):
        st = st_ref[h]
        o = _dot_nt(qs[:, hs], st.astype(BF16)) + _dot(amat[h].astype(BF16), vb[:, hs])
        st_ref[h] = st * decay[:, hs] + _dot_tn(vb[:, hs], ks[:, hs])
        ms = jnp.mean(o * o, axis=-1, keepdims=True)
        g = zg[:, hs]
        outs.append(o * lax.rsqrt(ms + RMS_EPS) * gn * (g * _sigmoid(g)))
    return jnp.concatenate(outs, axis=1)


def _lower_bound(lbraw_ref, layer):
    r = lbraw_ref[...]
    e = jnp.exp(r - jnp.max(r, axis=0, keepdims=True))
    sm = e / jnp.sum(e, axis=0, keepdims=True)
    cs = sm[0:1]
    for l in range(1, layer + 1):
        cs = cs + sm[l:l + 1]
    return jnp.maximum(cs - sm[0:1], LB_FLOOR)


def _hgrn_body(hq_ref, hf_ref, hv_ref, hg_ref, lbraw_ref, gn_ref, s0_ref, y_ref, sout_ref, st_ref, *,
               layer, rows_in, n_valid):
    c = HG_CHUNK
    lbf = _lower_bound(lbraw_ref, layer)
    gn = gn_ref[...]
    for h in range(HG_HEADS):
        st_ref[h] = s0_ref[0, h].T
    tq = lax.broadcasted_iota(I32, (c, 1), 0)
    if rows_in < c:
        pad = jnp.zeros((c - rows_in, HG_W), F32)

        def padded(ref):
            return jnp.concatenate([ref[...], pad], axis=0)

        y = _hgrn_chunk(padded(hq_ref), padded(hf_ref), padded(hv_ref), padded(hg_ref), tq < n_valid, lbf,
                        gn, st_ref)
        y_ref[...] = y[0:rows_in]
    else:
        def step(ci, carry):
            rows = pl.ds(pl.multiple_of(ci * c, c), c)
            y_ref[rows, :] = _hgrn_chunk(hq_ref[rows, :], hf_ref[rows, :], hv_ref[rows, :], hg_ref[rows, :],
                                         (ci * c + tq) < n_valid, lbf, gn, st_ref)
            return carry

        lax.fori_loop(0, rows_in // c, step, 0)
    for h in range(HG_HEADS):
        sout_ref[0, h] = st_ref[h].T


def hgrn(z, lb_raw, hg_norm, s0, y_prev, *, layer, row0, n_seq, rows_in, n_valid, out_rows):
    rb = row0 // rows_in
    depth = lb_raw.shape[0]
    body = functools.partial(_hgrn_body, layer=layer, rows_in=rows_in, n_valid=n_valid)
    cols = [Z_HQ // HG_W, Z_HF // HG_W, Z_HV // HG_W, Z_HG // HG_W]
    zspec = [pl.BlockSpec((rows_in, HG_W), functools.partial(lambda b, cc: (rb + b, cc), cc=cc)) for cc in cols]
    operands = [z, z, z, z, lb_raw, hg_norm, s0]
    in_specs = zspec + [
        pl.BlockSpec((depth, HG_W), lambda b: (0, 0)),
        pl.BlockSpec((1, HG_DV), lambda b: (0, 0)),
        pl.BlockSpec((1, HG_HEADS, HG_DK, HG_DV), lambda b: (b, 0, 0, 0)),
    ]
    aliases = {}
    if y_prev is not None:
        aliases = {len(operands): 0}
        operands.append(y_prev)
        in_specs.append(pl.BlockSpec(memory_space=pl.ANY))
        body = functools.partial(_hgrn_body_aliased, body)
    return pl.pallas_call(
        body,
        grid=(n_seq,),
        in_specs=in_specs,
        out_specs=[
            pl.BlockSpec((rows_in, HG_W), lambda b: (rb + b, 0)),
            pl.BlockSpec((1, HG_HEADS, HG_DK, HG_DV), lambda b: (b, 0, 0, 0)),
        ],
        out_shape=[
            jax.ShapeDtypeStruct((out_rows, HG_W), F32),
            jax.ShapeDtypeStruct((n_seq, HG_HEADS, HG_DK, HG_DV), F32),
        ],
        scratch_shapes=[pltpu.VMEM((HG_HEADS, HG_DV, HG_DK), F32)],
        input_output_aliases=aliases,
        compiler_params=_params(("parallel",)),
        name="hgrn",
    )(*operands)


def _hgrn_body_aliased(body, hq_ref, hf_ref, hv_ref, hg_ref, lbraw_ref, gn_ref, s0_ref, y_prev_hbm, *rest):
    del y_prev_hbm
    body(hq_ref, hf_ref, hv_ref, hg_ref, lbraw_ref, gn_ref, s0_ref, *rest)


def _permute_w_in(w):
    o = ORIG_OFFS
    parts = [w[:, o["q"]:o["q"] + 2048],
             w[:, o["hq"]:o["hq"] + 2048],
             w[:, o["ga"]:o["ga"] + 2048],
             w[:, o["ki"]:o["ki"] + IDX_DIM],
             w[:, o["wi"]:o["wi"] + IDX_HEADS],
             jnp.zeros((w.shape[0], Z_W - Z_WI - IDX_HEADS), w.dtype)]
    return jnp.concatenate(parts, axis=1).astype(BF16)


ROW_TILE = 768
FF_TILE = D_FF // 2


def kernel(x_prompt, x_sample, cache_k, cache_v, cache_kidx, state_hgrn, page_table, meta_tokens, norm_gains,
           ffn1_w_gu, ffn1_w_down, w_in, lb_raw, hg_norm, w_branch_attn, w_branch_hgrn, w_out, ffn2_w_gu,
           ffn2_w_down):
    return _forward(x_prompt, x_sample, cache_k, cache_v, cache_kidx, state_hgrn, page_table, meta_tokens,
                    norm_gains, ffn1_w_gu, ffn1_w_down, w_in, lb_raw, hg_norm, w_branch_attn, w_branch_hgrn,
                    w_out, ffn2_w_gu, ffn2_w_down, tm=ROW_TILE)


def _forward(x_prompt, x_sample, cache_k, cache_v, cache_kidx, state_hgrn, page_table, meta_tokens, norm_gains,
             ffn1_w_gu, ffn1_w_down, w_in, lb_raw, hg_norm, w_branch_attn, w_branch_hgrn, w_out, ffn2_w_gu,
             ffn2_w_down, *, tm):
    n_seq, seq, d = x_prompt.shape
    db, ds, _ = x_sample.shape
    depth = w_in.shape[0]
    t_real = seq + N_META
    t_pad = -(-t_real // QB) * QB
    rows_p = n_seq * t_pad
    rows = rows_p + db * ds
    assert rows % tm == 0
    topk_p = min(TOPK, t_real // 4)
    topk_s = min(TOPK, (page_table.shape[1] * PAGE + ds) // 4)

    meta = jnp.broadcast_to(meta_tokens[None].astype(F32), (n_seq, N_META, d))
    xp = jnp.concatenate([meta, x_prompt, jnp.zeros((n_seq, t_pad - t_real, d), F32)], axis=1)
    x = jnp.concatenate([xp.reshape(rows_p, d), x_sample.reshape(db * ds, d)], axis=0)

    ck = cache_k.transpose(0, 1, 3, 4, 2)
    cv = cache_v.transpose(0, 1, 3, 4, 2)
    cki = cache_kidx.transpose(0, 1, 3, 2)
    zero_state = jnp.zeros((n_seq, HG_HEADS, HG_DK, HG_DV), F32)
    y_zero = jnp.zeros((rows, ATT_W), F32)
    hgn = hg_norm.reshape(depth, 1, HG_DV)

    outs = {k: [] for k in ("kp", "vp", "kip", "sp", "ks", "vs", "kis", "ss")}
    for l in range(depth):
        g = norm_gains[l].reshape(6, 1, d)
        x = ffn(x, g[0], g[1], ffn1_w_gu[l].astype(BF16), ffn1_w_down[l].astype(BF16), tm=tm, tf=FF_TILE)
        z = inproj(x, g[2], _permute_w_in(w_in[l]), tm=tm, tn=896)
        ya = attn_prompt(z, y_zero, n_seq=n_seq, t_pad=t_pad, topk=topk_p)
        ya = attn_sample(z, page_table, cki, ck, cv, ya, layer=l, row0=rows_p, n_seq=db, ds=ds, topk=topk_s)
        yh, st_p = hgrn(z, lb_raw, hgn[l], zero_state, y_zero, layer=l, row0=0, n_seq=n_seq, rows_in=t_pad,
                        n_valid=t_real, out_rows=rows)
        yh, st_s = hgrn(z, lb_raw, hgn[l], state_hgrn[l], yh, layer=l, row0=rows_p, n_seq=db, rows_in=ds,
                        n_valid=ds, out_rows=rows)
        x = combine(x, ya, yh, z, w_branch_attn[l].astype(BF16), w_branch_hgrn[l].astype(BF16),
                    w_out[l].astype(BF16), g[3], tm=tm)
        x = ffn(x, g[4], g[5], ffn2_w_gu[l].astype(BF16), ffn2_w_down[l].astype(BF16), tm=tm, tf=FF_TILE)

        def cols_p(c0, w):
            return lax.slice(z, (0, c0), (rows_p, c0 + w)).reshape(n_seq, t_pad, w)[:, :t_real]

        def cols_s(c0, w):
            return lax.slice(z, (rows_p, c0), (rows, c0 + w)).reshape(db, ds, w)

        outs["kp"].append(cols_p(Z_K, ATT_W).reshape(n_seq, t_real, N_HEADS, HEAD_DIM))
        outs["vp"].append(cols_p(Z_V, ATT_W).reshape(n_seq, t_real, N_HEADS, HEAD_DIM))
        outs["kip"].append(cols_p(Z_KI, IDX_DIM))
        outs["sp"].append(st_p)
        outs["ks"].append(cols_s(Z_K, ATT_W).reshape(db, ds, N_HEADS, HEAD_DIM))
        outs["vs"].append(cols_s(Z_V, ATT_W).reshape(db, ds, N_HEADS, HEAD_DIM))
        outs["kis"].append(cols_s(Z_KI, IDX_DIM))
        outs["ss"].append(st_s)

    y_prompt = x[:rows_p].reshape(n_seq, t_pad, d)[:, N_META:t_real]
    y_sample = x[rows_p:].reshape(db, ds, d)
    st = lambda k: jnp.stack(outs[k])
    return (y_prompt, y_sample, st("kp"), st("vp"), st("kip"), st("sp"), st("ks"), st("vs"), st("kis"),
            st("ss"))
```

```python
import functools

import jax
import jax.numpy as jnp
from jax import lax
from jax.experimental import pallas as pl
from jax.experimental.pallas import tpu as pltpu

F32, BF16, I32 = jnp.float32, jnp.bfloat16, jnp.int32

D_MODEL = 1024
N_META = 16
N_HEADS = 8
HEAD_DIM = 64
ATT_W = N_HEADS * HEAD_DIM
IDX_HEADS = 8
IDX_DIM = 64
TOPK = 256
HG_HEADS = 4
HG_DK = 128
HG_DV = 128
HG_W = HG_HEADS * HG_DK
D_FF = 2816
PAGE = 128
RMS_EPS = 1e-6
NEG_INF = -1e30
LB_FLOOR = 1e-30
INT_MIN = -(2 ** 31)

LANES = 128
SUBLANES = 8
VMEM_LIMIT = 56 * 1024 * 1024

Z_Q, Z_K, Z_V, Z_QI = 0, 512, 1024, 1536
Z_HQ, Z_HF, Z_HV, Z_HG = 2048, 2560, 3072, 3584
Z_GA, Z_GH = 4096, 5120
Z_KI, Z_WI = 6144, 6208
Z_W = 6272
ORIG_OFFS = dict(q=0, k=512, v=1024, qi=1536, wi=2048, ki=2056, hq=2120, hf=2632, hv=3144,
                 hg=3656, ga=4168, gh=5192)

QB = 128
KV_STEP = 512
HG_CHUNK = 128


def _dot(a, b):
    return jnp.dot(a, b, preferred_element_type=F32)


def _dot_nt(a, b):
    return lax.dot_general(a, b, (((1,), (1,)), ((), ())), preferred_element_type=F32)


def _dot_tn(a, b):
    return lax.dot_general(a, b, (((0,), (0,)), ((), ())), preferred_element_type=F32)


def _rms(x, g):
    ms = jnp.mean(x * x, axis=-1, keepdims=True)
    return x * lax.rsqrt(ms + RMS_EPS) * g


def _sigmoid(x):
    return 1.0 / (1.0 + jnp.exp(-x))


def _params(sem):
    return pltpu.CompilerParams(dimension_semantics=sem, vmem_limit_bytes=VMEM_LIMIT)


def _ffn_body(x_ref, gpre_ref, gpost_ref, wg_ref, wu_ref, wd_ref, o_ref, h_ref, acc_ref):
    j = pl.program_id(1)

    @pl.when(j == 0)
    def _():
        h_ref[...] = _rms(x_ref[...], gpre_ref[...]).astype(BF16)
        acc_ref[...] = jnp.zeros_like(acc_ref)

    h = h_ref[...]
    g = _dot(h, wg_ref[...])
    u = _dot(h, wu_ref[...])
    a = (g * _sigmoid(g) * u).astype(BF16)
    acc_ref[...] += _dot(a, wd_ref[...])

    @pl.when(j == pl.num_programs(1) - 1)
    def _():
        o_ref[...] = x_ref[...] + 0.5 * _rms(acc_ref[...], gpost_ref[...])


def ffn(x, gpre, gpost, w_gu, w_dn, *, tm, tf):
    rows, d = x.shape
    ff = w_dn.shape[0]
    nj = ff // tf
    return pl.pallas_call(
        _ffn_body,
        grid=(rows // tm, nj),
        in_specs=[
            pl.BlockSpec((tm, d), lambda i, j: (i, 0)),
            pl.BlockSpec((1, d), lambda i, j: (0, 0)),
            pl.BlockSpec((1, d), lambda i, j: (0, 0)),
            pl.BlockSpec((d, tf), lambda i, j: (0, j)),
            pl.BlockSpec((d, tf), lambda i, j: (0, j + nj)),
            pl.BlockSpec((tf, d), lambda i, j: (j, 0)),
        ],
        out_specs=pl.BlockSpec((tm, d), lambda i, j: (i, 0)),
        out_shape=jax.ShapeDtypeStruct((rows, d), F32),
        scratch_shapes=[pltpu.VMEM((tm, d), BF16), pltpu.VMEM((tm, d), F32)],
        compiler_params=_params(("parallel", "arbitrary")),
        name="ffn",
    )(x, gpre, gpost, w_gu, w_gu, w_dn)


def _inproj_body(x_ref, g_ref, w_ref, o_ref, h_ref):
    @pl.when(pl.program_id(1) == 0)
    def _():
        h_ref[...] = _rms(x_ref[...], g_ref[...]).astype(BF16)

    o_ref[...] = _dot(h_ref[...], w_ref[...])


def inproj(x, g, w, *, tm, tn):
    rows, d = x.shape
    n = w.shape[1]
    return pl.pallas_call(
        _inproj_body,
        grid=(rows // tm, n // tn),
        in_specs=[
            pl.BlockSpec((tm, d), lambda i, j: (i, 0)),
            pl.BlockSpec((1, d), lambda i, j: (0, 0)),
            pl.BlockSpec((d, tn), lambda i, j: (0, j)),
        ],
        out_specs=pl.BlockSpec((tm, tn), lambda i, j: (i, j)),
        out_shape=jax.ShapeDtypeStruct((rows, n), F32),
        scratch_shapes=[pltpu.VMEM((tm, d), BF16)],
        compiler_params=_params(("parallel", "arbitrary")),
        name="inproj",
    )(x, g, w)


def _combine_body(x_ref, ya_ref, yh_ref, ga_ref, gh_ref, wpa_ref, wph_ref, wo_ref, g_ref, o_ref):
    a = _dot(ya_ref[...].astype(BF16), wpa_ref[...])
    hh = _dot(yh_ref[...].astype(BF16), wph_ref[...])
    m = _sigmoid(ga_ref[...]) * a + _sigmoid(gh_ref[...]) * hh
    r = _dot(m.astype(BF16), wo_ref[...])
    o_ref[...] = x_ref[...] + _rms(r, g_ref[...])


def combine(x, ya, yh, z, w_pa, w_ph, w_o, g, *, tm):
    rows, d = x.shape
    return pl.pallas_call(
        _combine_body,
        grid=(rows // tm,),
        in_specs=[
            pl.BlockSpec((tm, d), lambda i: (i, 0)),
            pl.BlockSpec((tm, ATT_W), lambda i: (i, 0)),
            pl.BlockSpec((tm, HG_W), lambda i: (i, 0)),
            pl.BlockSpec((tm, d), lambda i: (i, Z_GA // D_MODEL)),
            pl.BlockSpec((tm, d), lambda i: (i, Z_GH // D_MODEL)),
            pl.BlockSpec((ATT_W, d), lambda i: (0, 0)),
            pl.BlockSpec((HG_W, d), lambda i: (0, 0)),
            pl.BlockSpec((d, d), lambda i: (0, 0)),
            pl.BlockSpec((1, d), lambda i: (0, 0)),
        ],
        out_specs=pl.BlockSpec((tm, d), lambda i: (i, 0)),
        out_shape=jax.ShapeDtypeStruct((rows, d), F32),
        compiler_params=_params(("parallel",)),
        name="combine",
    )(x, ya, yh, z, z, w_pa, w_ph, w_o, g)


def _sortable_key(score, adm):
    s = jnp.where(score == 0.0, 0.0, score)
    b = lax.bitcast_convert_type(s, I32)
    key = b ^ ((b >> 31) & 0x7FFFFFFF)
    return jnp.where(adm, key, INT_MIN)


HALF_BIAS = 32768
PACK16 = 16
N_ACC = 4


def _topk_mask(key_ref, k16_ref, n, kpos, topk, idx_bits, axis):
    kf = float(topk)
    view = (lambda r: r[:n, :]) if axis == 0 else (lambda r: r[:, :n])
    tile = PACK16 if axis == 0 else LANES

    def tile_of(j):
        return k16_ref[j * tile:(j + 1) * tile, :] if axis == 0 else k16_ref[:, j * tile:(j + 1) * tile]

    def count(pred):
        return jnp.sum(jnp.where(pred, 1.0, 0.0), axis=axis, keepdims=True)

    def count16(cand16):
        accs = [None] * N_ACC
        for j in range(n // tile):
            one = jnp.where(tile_of(j) >= cand16, jnp.int16(1), jnp.int16(0))
            accs[j % N_ACC] = one if accs[j % N_ACC] is None else accs[j % N_ACC] + one
        tot = accs[0]
        for a in accs[1:]:
            if a is not None:
                tot = tot + a
        return jnp.sum(tot.astype(F32), axis=axis, keepdims=True)

    def search16(need):
        def step(it, thr_u):
            cand_u = thr_u | jnp.left_shift(jnp.int32(1), 15 - it)
            cnt = count16((cand_u - HALF_BIAS).astype(jnp.int16))
            return jnp.where(cnt >= need, cand_u, thr_u)

        shape = (1, key_ref.shape[1]) if axis == 0 else (key_ref.shape[0], 1)
        return lax.fori_loop(0, 16, step, jnp.zeros(shape, I32))

    key = view(key_ref)
    hi = key >> 16
    if axis == 0:
        k16_ref[:n, :] = hi.astype(jnp.int16)
    else:
        k16_ref[:, :n] = hi.astype(jnp.int16)
    thr_hi = search16(kf) - HALF_BIAS
    need_lo = kf - count(hi > thr_hi)
    lo = jnp.where(hi == thr_hi, (key & 0xFFFF) - HALF_BIAS, -HALF_BIAS).astype(jnp.int16)
    if axis == 0:
        k16_ref[:n, :] = lo
    else:
        k16_ref[:, :n] = lo
    thr = (thr_hi << 16) | search16(need_lo)

    gt = key > thr
    eq = key == thr
    need = kf - count(gt)
    tie = ((count(eq) > need) & (thr != INT_MIN)).astype(F32)

    def index_search():
        def step(it, bound):
            cand = bound | jnp.left_shift(jnp.int32(1), idx_bits - 1 - it)
            c = count((view(key_ref) == thr) & (kpos < cand))
            return jnp.where(c <= need, cand, bound)

        return lax.fori_loop(0, idx_bits, step, jnp.zeros(thr.shape, I32))

    bound = lax.cond(jnp.max(tie) > 0.5, index_search, lambda: jnp.full(thr.shape, 1 << idx_bits, I32))
    return (gt | (eq & (kpos < bound))) & (key != INT_MIN)


N_PAIRS = N_HEADS // 2


def _attn_prompt_variant(lk, qp_ref, qip_ref, w8t_ref, kb_ref, vb_ref, ki2_ref, ob_ref, sc_ref, key_ref, k16_ref,
                         bias_ref, topk, idx_bits):
    i = pl.program_id(1)
    low = lax.broadcasted_iota(I32, (QB, LANES), 1) < HEAD_DIM

    qpos = i * QB + lax.broadcasted_iota(I32, (1, QB), 1)
    kpos = lax.broadcasted_iota(I32, (lk, 1), 0)
    sc_ref[:lk, :] = jnp.zeros((lk, QB), F32)

    def idx_pair(j, carry):
        d = _dot_nt(ki2_ref[0:lk, :], qip_ref[j])
        w0 = w8t_ref[pl.ds(2 * j, 1), :]
        w1 = w8t_ref[pl.ds(2 * j + 1, 1), :]
        sc_ref[:lk, :] += w0 * jnp.maximum(d[:, :QB], 0.0) + w1 * jnp.maximum(d[:, QB:], 0.0)
        return carry

    lax.fori_loop(0, N_PAIRS, idx_pair, 0)
    key_ref[:lk, :] = _sortable_key(sc_ref[:lk, :], kpos <= qpos)
    sel = _topk_mask(key_ref, k16_ref, lk, kpos, topk, idx_bits, axis=0)
    bias_ref[:, :lk] = jnp.where(sel, 0.0, NEG_INF).T

    def att_pair(j, carry):
        s2 = _dot_nt(qp_ref[j], kb_ref[j, 0:lk, :])
        ps, ls = [], []
        for half in range(2):
            s = s2[half * QB:(half + 1) * QB] + bias_ref[:, :lk]
            m = jnp.max(s, axis=1, keepdims=True)
            p = jnp.exp(s - m)
            ls.append(jnp.sum(p, axis=1, keepdims=True))
            ps.append(p.astype(BF16))
        pv = _dot(jnp.concatenate(ps, axis=0), vb_ref[j, 0:lk, :])
        ob_ref[j] = jnp.where(low, pv[:QB] / ls[0], pv[QB:] / ls[1])
        return carry

    lax.fori_loop(0, N_PAIRS, att_pair, 0)


def _attn_prompt_body(q_ref, qi_ref, kwq_ref, k_ref, v_ref, kwk_ref, y_init_hbm, o_ref, kb_ref, vb_ref, ki2_ref,
                      qp_ref, qip_ref, w8t_ref, ob_ref, sc_ref, key_ref, k16_ref, bias_ref, *, t_pad, topk, idx_bits):
    i = pl.program_id(1)
    low = lax.broadcasted_iota(I32, (QB, LANES), 1) < HEAD_DIM

    def split_pair(tile):
        zero = jnp.zeros_like(tile)
        return jnp.concatenate([jnp.where(low, tile, zero), jnp.where(low, zero, tile)], axis=0)

    @pl.when(i == 0)
    def _():
        for j in range(N_PAIRS):
            ts = slice(j * LANES, (j + 1) * LANES)
            kb_ref[j] = k_ref[:, ts].astype(BF16)
            vb_ref[j] = v_ref[:, ts].astype(BF16)
        kw = kwk_ref[...]
        first = lax.broadcasted_iota(I32, kw.shape, 1) < IDX_DIM
        ki2_ref[...] = jnp.where(first, kw, pltpu.roll(kw, IDX_DIM, 1)).astype(BF16)

    q = (q_ref[...] * (HEAD_DIM ** -0.5)).astype(BF16)
    qi = qi_ref[...].astype(BF16)
    for j in range(N_PAIRS):
        ts = slice(j * LANES, (j + 1) * LANES)
        qp_ref[j] = split_pair(q[:, ts])
        qip_ref[j] = split_pair(qi[:, ts])
    w8t_ref[...] = kwq_ref[...].T[IDX_DIM:IDX_DIM + IDX_HEADS, :] * ((IDX_HEADS ** -0.5) * (IDX_DIM ** -0.5))

    nqb = t_pad // QB
    per = KV_STEP // QB
    lo = 0
    while lo < nqb:
        hi = min(lo + per, nqb)
        lk = hi * QB

        @pl.when((i >= lo) & (i < hi))
        def _(lk=lk):
            _attn_prompt_variant(lk, qp_ref, qip_ref, w8t_ref, kb_ref, vb_ref, ki2_ref, ob_ref, sc_ref, key_ref,
                                 k16_ref, bias_ref, topk, idx_bits)

        lo = hi
    for j in range(N_PAIRS):
        o_ref[:, j * LANES:(j + 1) * LANES] = ob_ref[j]


def attn_prompt(z, y_init, *, n_seq, t_pad, topk):
    nqb = t_pad // QB
    idx_bits = max(1, (t_pad - 1).bit_length())
    cq, ck, cv, cqi = Z_Q // ATT_W, Z_K // ATT_W, Z_V // ATT_W, Z_QI // ATT_W
    ckw = Z_KI // LANES
    body = functools.partial(_attn_prompt_body, t_pad=t_pad, topk=topk, idx_bits=idx_bits)
    return pl.pallas_call(
        body,
        grid=(n_seq, nqb),
        in_specs=[
            pl.BlockSpec((QB, ATT_W), lambda b, i: (b * nqb + i, cq)),
            pl.BlockSpec((QB, ATT_W), lambda b, i: (b * nqb + i, cqi)),
            pl.BlockSpec((QB, LANES), lambda b, i: (b * nqb + i, ckw)),
            pl.BlockSpec((t_pad, ATT_W), lambda b, i: (b, ck)),
            pl.BlockSpec((t_pad, ATT_W), lambda b, i: (b, cv)),
            pl.BlockSpec((t_pad, LANES), lambda b, i: (b, ckw)),
            pl.BlockSpec(memory_space=pl.ANY),
        ],
        out_specs=pl.BlockSpec((QB, ATT_W), lambda b, i: (b * nqb + i, 0)),
        out_shape=jax.ShapeDtypeStruct(y_init.shape, F32),
        input_output_aliases={6: 0},
        scratch_shapes=[pltpu.VMEM((N_PAIRS, t_pad, LANES), BF16), pltpu.VMEM((N_PAIRS, t_pad, LANES), BF16),
                        pltpu.VMEM((t_pad, LANES), BF16),
                        pltpu.VMEM((N_PAIRS, 2 * QB, LANES), BF16), pltpu.VMEM((N_PAIRS, 2 * QB, LANES), BF16),
                        pltpu.VMEM((IDX_HEADS, QB), F32), pltpu.VMEM((N_PAIRS, QB, LANES), F32),
                        pltpu.VMEM((t_pad, QB), F32),
                        pltpu.VMEM((t_pad, QB), I32), pltpu.VMEM((t_pad, QB), jnp.int16),
                        pltpu.VMEM((QB, t_pad), F32)],
        compiler_params=_params(("parallel", "arbitrary")),
        name="attn_prompt",
    )(z, z, z, z, z, z, y_init)


IDX_PAGES = 16
KV_PAGES = 8
KV_SLOTS = 3


SEL_SEQS = 8


def _sample_select_body(pt_ref, qi_ref, kwn_ref, ckidx_hbm, mask_ref, kibuf, sc_ref, key_ref, k16_ref, sem_ki, *,
                        layer, n_pages, ds, topk, idx_bits):
    g = pl.program_id(0)
    past = n_pages * PAGE
    ltot = past + LANES
    n_ch = n_pages // IDX_PAGES
    rows = sc_ref.shape[0]
    sel_seqs = rows // ds
    n_steps = sel_seqs * n_ch
    width = IDX_PAGES * PAGE

    def ki_copy(step, slot, p):
        seq = g * sel_seqs + step // n_ch
        page = pt_ref[seq, (step % n_ch) * IDX_PAGES + p]
        return pltpu.make_async_copy(ckidx_hbm.at[layer, page], kibuf.at[slot, p], sem_ki.at[slot])

    def start_ki(step, slot):
        for p in range(IDX_PAGES):
            ki_copy(step, slot, p).start()

    def wait_ki(step, slot):
        for p in range(IDX_PAGES):
            ki_copy(step, slot, p).wait()

    start_ki(0, 0)

    def seq_operands(s):
        r0 = s * ds if isinstance(s, int) else pl.multiple_of(s * ds, ds)
        qi = qi_ref[pl.ds(r0, ds), :]
        kw = kwn_ref[pl.ds(r0, ds), :]
        qh = jnp.concatenate([qi[:, h * IDX_DIM:(h + 1) * IDX_DIM] for h in range(IDX_HEADS)], axis=0)
        w8 = kw[:, IDX_DIM:IDX_DIM + IDX_HEADS] * (IDX_HEADS ** -0.5)
        wcol = jnp.concatenate([w8[:, h:h + 1] for h in range(IDX_HEADS)], axis=0)
        return r0, qh.astype(BF16), wcol, kw[:, 0:IDX_DIM]

    def idx_scores(wcol, d):
        r = wcol * jnp.maximum(d * (IDX_DIM ** -0.5), 0.0)
        acc = r[0:ds]
        for h in range(1, IDX_HEADS):
            acc = acc + r[h * ds:(h + 1) * ds]
        return acc

    def idx_step(step, carry):
        slot = step % 2

        @pl.when(step + 1 < n_steps)
        def _():
            start_ki(step + 1, 1 - slot)

        wait_ki(step, slot)
        r0, qh, wcol, _ = seq_operands(step // n_ch)
        kit = jnp.concatenate([kibuf[slot, p] for p in range(IDX_PAGES)], axis=1).astype(BF16)
        cols = pl.ds(pl.multiple_of((step % n_ch) * width, width), width)
        sc_ref[pl.ds(r0, ds), cols] = idx_scores(wcol, _dot(qh, kit))
        return carry

    lax.fori_loop(0, n_steps, idx_step, 0)
    for s in range(sel_seqs):
        r0, qh, wcol, ki_new = seq_operands(s)
        ki_pad = jnp.concatenate([ki_new, jnp.zeros((LANES - ds, IDX_DIM), F32)], axis=0).astype(BF16)
        sc_ref[r0:r0 + ds, past:ltot] = idx_scores(wcol, _dot_nt(qh, ki_pad))

    qpos = past + lax.broadcasted_iota(I32, (rows, 1), 0) % ds
    kpos = lax.broadcasted_iota(I32, (1, ltot), 1)
    key_ref[...] = _sortable_key(sc_ref[...], kpos <= qpos)
    sel = _topk_mask(key_ref, k16_ref, ltot, kpos, topk, idx_bits, axis=1)
    mask_ref[...] = jnp.where(sel, 1.0, 0.0)


def _sample_attend_body(pt_ref, q_ref, kn_ref, vn_ref, mask_ref, ck_hbm, cv_hbm, ya_hbm, o_ref, kbuf, vbuf, sem_k, sem_v,
                        *, layer, n_pages, ds):
    b = pl.program_id(0)
    past = n_pages * PAGE
    ltot = past + LANES
    n_ch = n_pages // KV_PAGES
    rows_h = N_HEADS * ds

    n_total = pl.num_programs(0) * n_ch

    def kv_copy(g, p):
        slot = g % KV_SLOTS
        page = pt_ref[g // n_ch, (g % n_ch) * KV_PAGES + p]
        return (pltpu.make_async_copy(ck_hbm.at[layer, page], kbuf.at[slot, p], sem_k.at[slot]),
                pltpu.make_async_copy(cv_hbm.at[layer, page], vbuf.at[slot, p], sem_v.at[slot]))

    def start_kv(g):
        for p in range(KV_PAGES):
            ck, cv = kv_copy(g, p)
            ck.start()
            cv.start()

    def wait_kv(g):
        for p in range(KV_PAGES):
            ck, cv = kv_copy(g, p)
            ck.wait()
            cv.wait()

    @pl.when(b == 0)
    def _():
        for g in range(KV_SLOTS - 1):
            start_kv(g)

    q8 = jnp.concatenate([q_ref[...] * (HEAD_DIM ** -0.5)] * N_HEADS, axis=0)
    rhead = lax.broadcasted_iota(I32, (rows_h, ATT_W), 0) // ds
    chead = lax.broadcasted_iota(I32, (rows_h, ATT_W), 1) // HEAD_DIM
    qbd = jnp.where(rhead == chead, q8, 0.0).astype(BF16)

    def attend(s, msk, pv_fn, state):
        m_old, l_old, acc = state
        on = jnp.concatenate([msk] * N_HEADS, axis=0) > 0.5
        s = jnp.where(on, s, NEG_INF)
        m_new = jnp.maximum(m_old, jnp.max(s, axis=1, keepdims=True))
        alpha = jnp.exp(m_old - m_new)
        p = jnp.where(on, jnp.exp(s - m_new), 0.0)
        l_new = alpha * l_old + jnp.sum(p, axis=1, keepdims=True)
        return m_new, l_new, alpha * acc + pv_fn(p.astype(BF16))

    def kv_step(c, state):
        g = b * n_ch + c
        slot = g % KV_SLOTS
        ahead = g + (KV_SLOTS - 1)

        @pl.when(ahead < n_total)
        def _():
            start_kv(ahead)

        wait_kv(g)
        width = KV_PAGES * PAGE
        s = jnp.concatenate(
            [_dot(qbd, kbuf[slot, p].reshape(ATT_W, PAGE).astype(BF16)) for p in range(KV_PAGES)], axis=1)

        def pv_fn(p):
            acc = _dot_nt(p[:, 0:PAGE], vbuf[slot, 0].reshape(ATT_W, PAGE).astype(BF16))
            for j in range(1, KV_PAGES):
                acc = acc + _dot_nt(p[:, j * PAGE:(j + 1) * PAGE],
                                    vbuf[slot, j].reshape(ATT_W, PAGE).astype(BF16))
            return acc

        msk = mask_ref[:, pl.ds(pl.multiple_of(c * width, width), width)]
        return attend(s, msk, pv_fn, state)

    state = (jnp.full((rows_h, 1), NEG_INF, F32), jnp.zeros((rows_h, 1), F32),
             jnp.zeros((rows_h, ATT_W), F32))
    state = lax.fori_loop(0, n_ch, kv_step, state)
    pad = jnp.zeros((LANES - ds, ATT_W), F32)
    kn = jnp.concatenate([kn_ref[...], pad], axis=0).astype(BF16)
    vn = jnp.concatenate([vn_ref[...], pad], axis=0).astype(BF16)
    _, l_fin, acc = attend(_dot_nt(qbd, kn), mask_ref[:, past:ltot], lambda p: _dot(p, vn), state)
    out = acc / l_fin
    for h in range(N_HEADS):
        hs = slice(h * HEAD_DIM, (h + 1) * HEAD_DIM)
        o_ref[:, hs] = out[h * ds:(h + 1) * ds, hs]


def attn_sample(z, page_table, cache_kidx, cache_k, cache_v, ya, *, layer, row0, n_seq, ds, topk):
    n_pages = page_table.shape[1]
    past = n_pages * PAGE
    ltot = past + LANES
    idx_bits = max(1, (ltot - 1).bit_length())
    cq, ck, cv, cqi = Z_Q // ATT_W, Z_K // ATT_W, Z_V // ATT_W, Z_QI // ATT_W
    ckw = Z_KI // LANES
    sel_seqs = min(SEL_SEQS, n_seq)
    rows = sel_seqs * ds
    rbs = row0 // rows
    assert n_seq % sel_seqs == 0 and row0 % rows == 0
    sel_body = functools.partial(_sample_select_body, layer=layer, n_pages=n_pages, ds=ds, topk=topk,
                                 idx_bits=idx_bits)
    mask = pl.pallas_call(
        sel_body,
        grid_spec=pltpu.PrefetchScalarGridSpec(
            num_scalar_prefetch=1,
            grid=(n_seq // sel_seqs,),
            in_specs=[
                pl.BlockSpec((rows, ATT_W), lambda g, pt: (rbs + g, cqi)),
                pl.BlockSpec((rows, LANES), lambda g, pt: (rbs + g, ckw)),
                pl.BlockSpec(memory_space=pl.ANY),
            ],
            out_specs=pl.BlockSpec((rows, ltot), lambda g, pt: (g, 0)),
            scratch_shapes=[
                pltpu.VMEM((2, IDX_PAGES, IDX_DIM, PAGE), F32),
                pltpu.VMEM((rows, ltot), F32),
                pltpu.VMEM((rows, ltot), I32),
                pltpu.VMEM((rows, ltot), jnp.int16),
                pltpu.SemaphoreType.DMA((2,)),
            ],
        ),
        out_shape=jax.ShapeDtypeStruct((n_seq * ds, ltot), F32),
        compiler_params=_params(("arbitrary",)),
        name="sample_select",
    )(page_table, z, z, cache_kidx)

    rb = row0 // ds
    att_body = functools.partial(_sample_attend_body, layer=layer, n_pages=n_pages, ds=ds)
    return pl.pallas_call(
        att_body,
        grid_spec=pltpu.PrefetchScalarGridSpec(
            num_scalar_prefetch=1,
            grid=(n_seq,),
            in_specs=[
                pl.BlockSpec((ds, ATT_W), lambda b, pt: (rb + b, cq)),
                pl.BlockSpec((ds, ATT_W), lambda b, pt: (rb + b, ck)),
                pl.BlockSpec((ds, ATT_W), lambda b, pt: (rb + b, cv)),
                pl.BlockSpec((ds, ltot), lambda b, pt: (b, 0)),
                pl.BlockSpec(memory_space=pl.ANY),
                pl.BlockSpec(memory_space=pl.ANY),
                pl.BlockSpec(memory_space=pl.ANY),
            ],
            out_specs=pl.BlockSpec((ds, ATT_W), lambda b, pt: (rb + b, 0)),
            scratch_shapes=[
                pltpu.VMEM((KV_SLOTS, KV_PAGES, N_HEADS, HEAD_DIM, PAGE), F32),
                pltpu.VMEM((KV_SLOTS, KV_PAGES, N_HEADS, HEAD_DIM, PAGE), F32),
                pltpu.SemaphoreType.DMA((KV_SLOTS,)),
                pltpu.SemaphoreType.DMA((KV_SLOTS,)),
            ],
        ),
        out_shape=jax.ShapeDtypeStruct(ya.shape, F32),
        input_output_aliases={7: 0},
        compiler_params=_params(("arbitrary",)),
        name="sample_attend",
    )(page_table, z, z, z, mask, cache_k, cache_v, ya)


def _hgrn_chunk(zq, zf, vv, zg, valid, lbf, gn, st_ref):
    c = zq.shape[0]
    levels = c.bit_length() - 1
    sig = _sigmoid(zf)
    f = lbf + (1.0 - lbf) * sig
    logf = jnp.where(valid, jnp.log(f), 0.0)
    kk = jnp.where(valid, (1.0 - lbf) * _sigmoid(-zf), 0.0)
    qq = zq * _sigmoid(zq)
    vb = jnp.where(valid, vv, 0.0).astype(BF16)

    tq = lax.broadcasted_iota(I32, (c, 1), 0)
    ts = lax.broadcasted_iota(I32, (1, c), 1)
    tril = (ts <= tq).astype(BF16)
    hi = logf.astype(BF16)
    r1 = logf - hi.astype(F32)
    mid = r1.astype(BF16)
    lo = (r1 - mid.astype(F32)).astype(BF16)
    bcum = _dot(tril, hi) + _dot(tril, mid) + _dot(tril, lo)

    x = tq ^ ts
    heads = [slice(h * HG_DK, (h + 1) * HG_DK) for h in range(HG_HEADS)]
    qb0, kb0 = qq.astype(BF16), kk.astype(BF16)
    amat = [jnp.where(x == 0, _dot_nt(qb0[:, hs], kb0[:, hs]), 0.0) for hs in heads]
    gprev = jnp.where(tq == 0, 0.0, pltpu.roll(bcum, 1, 0))
    hend = bcum
    for lev in range(levels):
        m = 1 << lev
        qs = (qq * jnp.exp(bcum - gprev)).astype(BF16)
        ks = (kk * jnp.exp(hend - bcum)).astype(BF16)
        sib = ((x >> lev) == 1) & (ts < tq)
        for h, hs in enumerate(heads):
            amat[h] = amat[h] + jnp.where(sib, _dot_nt(qs[:, hs], ks[:, hs]), 0.0)
        upper = ((tq >> lev) & 1) == 1
        gprev = jnp.where(upper, pltpu.roll(gprev, m, 0), gprev)
        hend = jnp.where(upper, hend, pltpu.roll(hend, c - m, 0))
    qs = (qq * jnp.exp(bcum)).astype(BF16)
    ks = (kk * jnp.exp(hend - bcum)).astype(BF16)
    decay = jnp.exp(hend[0:1, :])
    outs = []
    for h, hs in enumerate(heads):
        st = st_ref[h]
        o = _dot_nt(qs[:, hs], st.astype(BF16)) + _dot(amat[h].astype(BF16), vb[:, hs])
        st_ref[h] = st * decay[:, hs] + _dot_tn(vb[:, hs], ks[:, hs])
        ms = jnp.mean(o * o, axis=-1, keepdims=True)
        g = zg[:, hs]
        outs.append(o * lax.rsqrt(ms + RMS_EPS) * gn * (g * _sigmoid(g)))
    return jnp.concatenate(outs, axis=1)


def _lower_bound(lbraw_ref, layer):
    r = lbraw_ref[...]
    e = jnp.exp(r - jnp.max(r, axis=0, keepdims=True))
    sm = e / jnp.sum(e, axis=0, keepdims=True)
    cs = sm[0:1]
    for l in range(1, layer + 1):
        cs = cs + sm[l:l + 1]
    return jnp.maximum(cs - sm[0:1], LB_FLOOR)


def _hgrn_body(hq_ref, hf_ref, hv_ref, hg_ref, lbraw_ref, gn_ref, s0_ref, y_ref, sout_ref, st_ref, *,
               layer, rows_in, n_valid):
    c = HG_CHUNK
    lbf = _lower_bound(lbraw_ref, layer)
    gn = gn_ref[...]
    for h in range(HG_HEADS):
        st_ref[h] = s0_ref[0, h].T
    tq = lax.broadcasted_iota(I32, (c, 1), 0)
    if rows_in < c:
        pad = jnp.zeros((c - rows_in, HG_W), F32)

        def padded(ref):
            return jnp.concatenate([ref[...], pad], axis=0)

        y = _hgrn_chunk(padded(hq_ref), padded(hf_ref), padded(hv_ref), padded(hg_ref), tq < n_valid, lbf,
                        gn, st_ref)
        y_ref[...] = y[0:rows_in]
    else:
        def step(ci, carry):
            rows = pl.ds(pl.multiple_of(ci * c, c), c)
            y_ref[rows, :] = _hgrn_chunk(hq_ref[rows, :], hf_ref[rows, :], hv_ref[rows, :], hg_ref[rows, :],
                                         (ci * c + tq) < n_valid, lbf, gn, st_ref)
            return carry

        lax.fori_loop(0, rows_in // c, step, 0)
    for h in range(HG_HEADS):
        sout_ref[0, h] = st_ref[h].T


def hgrn(z, lb_raw, hg_norm, s0, y_prev, *, layer, row0, n_seq, rows_in, n_valid, out_rows):
    rb = row0 // rows_in
    depth = lb_raw.shape[0]
    body = functools.partial(_hgrn_body, layer=layer, rows_in=rows_in, n_valid=n_valid)
    cols = [Z_HQ // HG_W, Z_HF // HG_W, Z_HV // HG_W, Z_HG // HG_W]
    zspec = [pl.BlockSpec((rows_in, HG_W), functools.partial(lambda b, cc: (rb + b, cc), cc=cc)) for cc in cols]
    operands = [z, z, z, z, lb_raw, hg_norm, s0]
    in_specs = zspec + [
        pl.BlockSpec((depth, HG_W), lambda b: (0, 0)),
        pl.BlockSpec((1, HG_DV), lambda b: (0, 0)),
        pl.BlockSpec((1, HG_HEADS, HG_DK, HG_DV), lambda b: (b, 0, 0, 0)),
    ]
    aliases = {}
    if y_prev is not None:
        aliases = {len(operands): 0}
        operands.append(y_prev)
        in_specs.append(pl.BlockSpec(memory_space=pl.ANY))
        body = functools.partial(_hgrn_body_aliased, body)
    return pl.pallas_call(
        body,
        grid=(n_seq,),
        in_specs=in_specs,
        out_specs=[
            pl.BlockSpec((rows_in, HG_W), lambda b: (rb + b, 0)),
            pl.BlockSpec((1, HG_HEADS, HG_DK, HG_DV), lambda b: (b, 0, 0, 0)),
        ],
        out_shape=[
            jax.ShapeDtypeStruct((out_rows, HG_W), F32),
            jax.ShapeDtypeStruct((n_seq, HG_HEADS, HG_DK, HG_DV), F32),
        ],
        scratch_shapes=[pltpu.VMEM((HG_HEADS, HG_DV, HG_DK), F32)],
        input_output_aliases=aliases,
        compiler_params=_params(("parallel",)),
        name="hgrn",
    )(*operands)


def _hgrn_body_aliased(body, hq_ref, hf_ref, hv_ref, hg_ref, lbraw_ref, gn_ref, s0_ref, y_prev_hbm, *rest):
    del y_prev_hbm
    body(hq_ref, hf_ref, hv_ref, hg_ref, lbraw_ref, gn_ref, s0_ref, *rest)


def _permute_w_in(w):
    o = ORIG_OFFS
    parts = [w[:, o["q"]:o["q"] + 2048],
             w[:, o["hq"]:o["hq"] + 2048],
             w[:, o["ga"]:o["ga"] + 2048],
             w[:, o["ki"]:o["ki"] + IDX_DIM],
             w[:, o["wi"]:o["wi"] + IDX_HEADS],
             jnp.zeros((w.shape[0], Z_W - Z_WI - IDX_HEADS), w.dtype)]
    return jnp.concatenate(parts, axis=1).astype(BF16)


ROW_TILE = 768
FF_TILE = D_FF // 2


def kernel(x_prompt, x_sample, cache_k, cache_v, cache_kidx, state_hgrn, page_table, meta_tokens, norm_gains,
           ffn1_w_gu, ffn1_w_down, w_in, lb_raw, hg_norm, w_branch_attn, w_branch_hgrn, w_out, ffn2_w_gu,
           ffn2_w_down):
    return _forward(x_prompt, x_sample, cache_k, cache_v, cache_kidx, state_hgrn, page_table, meta_tokens,
                    norm_gains, ffn1_w_gu, ffn1_w_down, w_in, lb_raw, hg_norm, w_branch_attn, w_branch_hgrn,
                    w_out, ffn2_w_gu, ffn2_w_down, tm=ROW_TILE)


def _forward(x_prompt, x_sample, cache_k, cache_v, cache_kidx, state_hgrn, page_table, meta_tokens, norm_gains,
             ffn1_w_gu, ffn1_w_down, w_in, lb_raw, hg_norm, w_branch_attn, w_branch_hgrn, w_out, ffn2_w_gu,
             ffn2_w_down, *, tm):
    n_seq, seq, d = x_prompt.shape
    db, ds, _ = x_sample.shape
    depth = w_in.shape[0]
    t_real = seq + N_META
    t_pad = -(-t_real // QB) * QB
    rows_p = n_seq * t_pad
    rows = rows_p + db * ds
    assert rows % tm == 0
    topk_p = min(TOPK, t_real // 4)
    topk_s = min(TOPK, (page_table.shape[1] * PAGE + ds) // 4)

    meta = jnp.broadcast_to(meta_tokens[None].astype(F32), (n_seq, N_META, d))
    xp = jnp.concatenate([meta, x_prompt, jnp.zeros((n_seq, t_pad - t_real, d), F32)], axis=1)
    x = jnp.concatenate([xp.reshape(rows_p, d), x_sample.reshape(db * ds, d)], axis=0)

    ck = cache_k.transpose(0, 1, 3, 4, 2)
    cv = cache_v.transpose(0, 1, 3, 4, 2)
    cki = cache_kidx.transpose(0, 1, 3, 2)
    zero_state = jnp.zeros((n_seq, HG_HEADS, HG_DK, HG_DV), F32)
    y_zero = jnp.zeros((rows, ATT_W), F32)
    hgn = hg_norm.reshape(depth, 1, HG_DV)

    outs = {k: [] for k in ("kp", "vp", "kip", "sp", "ks", "vs", "kis", "ss")}
    for l in range(depth):
        g = norm_gains[l].reshape(6, 1, d)
        x = ffn(x, g[0], g[1], ffn1_w_gu[l].astype(BF16), ffn1_w_down[l].astype(BF16), tm=tm, tf=FF_TILE)
        z = inproj(x, g[2], _permute_w_in(w_in[l]), tm=tm, tn=896)
        ya = attn_prompt(z, y_zero, n_seq=n_seq, t_pad=t_pad, topk=topk_p)
        ya = attn_sample(z, page_table, cki, ck, cv, ya, layer=l, row0=rows_p, n_seq=db, ds=ds, topk=topk_s)
        yh, st_p = hgrn(z, lb_raw, hgn[l], zero_state, y_zero, layer=l, row0=0, n_seq=n_seq, rows_in=t_pad,
                        n_valid=t_real, out_rows=rows)
        yh, st_s = hgrn(z, lb_raw, hgn[l], state_hgrn[l], yh, layer=l, row0=rows_p, n_seq=db, rows_in=ds,
                        n_valid=ds, out_rows=rows)
        x = combine(x, ya, yh, z, w_branch_attn[l].astype(BF16), w_branch_hgrn[l].astype(BF16),
                    w_out[l].astype(BF16), g[3], tm=tm)
        x = ffn(x, g[4], g[5], ffn2_w_gu[l].astype(BF16), ffn2_w_down[l].astype(BF16), tm=tm, tf=FF_TILE)

        def cols_p(c0, w):
            return lax.slice(z, (0, c0), (rows_p, c0 + w)).reshape(n_seq, t_pad, w)[:, :t_real]

        def cols_s(c0, w):
            return lax.slice(z, (rows_p, c0), (rows, c0 + w)).reshape(db, ds, w)

        outs["kp"].append(cols_p(Z_K, ATT_W).reshape(n_seq, t_real, N_HEADS, HEAD_DIM))
        outs["vp"].append(cols_p(Z_V, ATT_W).reshape(n_seq, t_real, N_HEADS, HEAD_DIM))
        outs["kip"].append(cols_p(Z_KI, IDX_DIM))
        outs["sp"].append(st_p)
        outs["ks"].append(cols_s(Z_K, ATT_W).reshape(db, ds, N_HEADS, HEAD_DIM))
        outs["vs"].append(cols_s(Z_V, ATT_W).reshape(db, ds, N_HEADS, HEAD_DIM))
        outs["kis"].append(cols_s(Z_KI, IDX_DIM))
        outs["ss"].append(st_s)

    y_prompt = x[:rows_p].reshape(n_seq, t_pad, d)[:, N_META:t_real]
    y_sample = x[rows_p:].reshape(db, ds, d)
    st = lambda k: jnp.stack(outs[k])
    return (y_prompt, y_sample, st("kp"), st("vp"), st("kip"), st("sp"), st("ks"), st("vs"), st("kis"),
            st("ss"))
```

```python
import functools

import jax
import jax.numpy as jnp
from jax import lax
from jax.experimental import pallas as pl
from jax.experimental.pallas import tpu as pltpu

F32, BF16, I32 = jnp.float32, jnp.bfloat16, jnp.int32

D_MODEL = 1024
N_META = 16
N_HEADS = 8
HEAD_DIM = 64
ATT_W = N_HEADS * HEAD_DIM
IDX_HEADS = 8
IDX_DIM = 64
TOPK = 256
HG_HEADS = 4
HG_DK = 128
HG_DV = 128
HG_W = HG_HEADS * HG_DK
D_FF = 2816
PAGE = 128
RMS_EPS = 1e-6
NEG_INF = -1e30
LB_FLOOR = 1e-30
INT_MIN = -(2 ** 31)

LANES = 128
SUBLANES = 8
VMEM_LIMIT = 56 * 1024 * 1024

Z_Q, Z_K, Z_V, Z_QI = 0, 512, 1024, 1536
Z_HQ, Z_HF, Z_HV, Z_HG = 2048, 2560, 3072, 3584
Z_GA, Z_GH = 4096, 5120
Z_KI, Z_WI = 6144, 6208
Z_W = 6272
ORIG_OFFS = dict(q=0, k=512, v=1024, qi=1536, wi=2048, ki=2056, hq=2120, hf=2632, hv=3144,
                 hg=3656, ga=4168, gh=5192)

QB = 128
KV_STEP = 384
HG_CHUNK = 128


def _dot(a, b):
    return jnp.dot(a, b, preferred_element_type=F32)


def _dot_nt(a, b):
    return lax.dot_general(a, b, (((1,), (1,)), ((), ())), preferred_element_type=F32)


def _dot_tn(a, b):
    return lax.dot_general(a, b, (((0,), (0,)), ((), ())), preferred_element_type=F32)


def _rms(x, g):
    ms = jnp.mean(x * x, axis=-1, keepdims=True)
    return x * lax.rsqrt(ms + RMS_EPS) * g


def _sigmoid(x):
    return 1.0 / (1.0 + jnp.exp(-x))


def _params(sem):
    return pltpu.CompilerParams(dimension_semantics=sem, vmem_limit_bytes=VMEM_LIMIT)


def _ffn_body(x_ref, gpre_ref, gpost_ref, wg_ref, wu_ref, wd_ref, o_ref, h_ref, acc_ref):
    j = pl.program_id(1)

    @pl.when(j == 0)
    def _():
        h_ref[...] = _rms(x_ref[...], gpre_ref[...]).astype(BF16)
        acc_ref[...] = jnp.zeros_like(acc_ref)

    h = h_ref[...]
    g = _dot(h, wg_ref[...])
    u = _dot(h, wu_ref[...])
    a = (g * _sigmoid(g) * u).astype(BF16)
    acc_ref[...] += _dot(a, wd_ref[...])

    @pl.when(j == pl.num_programs(1) - 1)
    def _():
        o_ref[...] = x_ref[...] + 0.5 * _rms(acc_ref[...], gpost_ref[...])


def ffn(x, gpre, gpost, w_gu, w_dn, *, tm, tf):
    rows, d = x.shape
    ff = w_dn.shape[0]
    nj = ff // tf
    return pl.pallas_call(
        _ffn_body,
        grid=(rows // tm, nj),
        in_specs=[
            pl.BlockSpec((tm, d), lambda i, j: (i, 0)),
            pl.BlockSpec((1, d), lambda i, j: (0, 0)),
            pl.BlockSpec((1, d), lambda i, j: (0, 0)),
            pl.BlockSpec((d, tf), lambda i, j: (0, j)),
            pl.BlockSpec((d, tf), lambda i, j: (0, j + nj)),
            pl.BlockSpec((tf, d), lambda i, j: (j, 0)),
        ],
        out_specs=pl.BlockSpec((tm, d), lambda i, j: (i, 0)),
        out_shape=jax.ShapeDtypeStruct((rows, d), F32),
        scratch_shapes=[pltpu.VMEM((tm, d), BF16), pltpu.VMEM((tm, d), F32)],
        compiler_params=_params(("parallel", "arbitrary")),
        name="ffn",
    )(x, gpre, gpost, w_gu, w_gu, w_dn)


def _inproj_body(x_ref, g_ref, w_ref, o_ref, h_ref):
    @pl.when(pl.program_id(1) == 0)
    def _():
        h_ref[...] = _rms(x_ref[...], g_ref[...]).astype(BF16)

    o_ref[...] = _dot(h_ref[...], w_ref[...])


def inproj(x, g, w, *, tm, tn):
    rows, d = x.shape
    n = w.shape[1]
    return pl.pallas_call(
        _inproj_body,
        grid=(rows // tm, n // tn),
        in_specs=[
            pl.BlockSpec((tm, d), lambda i, j: (i, 0)),
            pl.BlockSpec((1, d), lambda i, j: (0, 0)),
            pl.BlockSpec((d, tn), lambda i, j: (0, j)),
        ],
        out_specs=pl.BlockSpec((tm, tn), lambda i, j: (i, j)),
        out_shape=jax.ShapeDtypeStruct((rows, n), F32),
        scratch_shapes=[pltpu.VMEM((tm, d), BF16)],
        compiler_params=_params(("parallel", "arbitrary")),
        name="inproj",
    )(x, g, w)


def _combine_body(x_ref, ya_ref, yh_ref, ga_ref, gh_ref, wpa_ref, wph_ref, wo_ref, g_ref, o_ref):
    a = _dot(ya_ref[...].astype(BF16), wpa_ref[...])
    hh = _dot(yh_ref[...].astype(BF16), wph_ref[...])
    m = _sigmoid(ga_ref[...]) * a + _sigmoid(gh_ref[...]) * hh
    r = _dot(m.astype(BF16), wo_ref[...])
    o_ref[...] = x_ref[...] + _rms(r, g_ref[...])


def combine(x, ya, yh, z, w_pa, w_ph, w_o, g, *, tm):
    rows, d = x.shape
    return pl.pallas_call(
        _combine_body,
        grid=(rows // tm,),
        in_specs=[
            pl.BlockSpec((tm, d), lambda i: (i, 0)),
            pl.BlockSpec((tm, ATT_W), lambda i: (i, 0)),
            pl.BlockSpec((tm, HG_W), lambda i: (i, 0)),
            pl.BlockSpec((tm, d), lambda i: (i, Z_GA // D_MODEL)),
            pl.BlockSpec((tm, d), lambda i: (i, Z_GH // D_MODEL)),
            pl.BlockSpec((ATT_W, d), lambda i: (0, 0)),
            pl.BlockSpec((HG_W, d), lambda i: (0, 0)),
            pl.BlockSpec((d, d), lambda i: (0, 0)),
            pl.BlockSpec((1, d), lambda i: (0, 0)),
        ],
        out_specs=pl.BlockSpec((tm, d), lambda i: (i, 0)),
        out_shape=jax.ShapeDtypeStruct((rows, d), F32),
        compiler_params=_params(("parallel",)),
        name="combine",
    )(x, ya, yh, z, z, w_pa, w_ph, w_o, g)


def _sortable_key(score, adm):
    s = jnp.where(score == 0.0, 0.0, score)
    b = lax.bitcast_convert_type(s, I32)
    key = b ^ ((b >> 31) & 0x7FFFFFFF)
    return jnp.where(adm, key, INT_MIN)


HALF_BIAS = 32768
PACK16 = 16
N_ACC = 4


def _topk_mask(key_ref, k16_ref, n, kpos, topk, idx_bits, axis):
    kf = float(topk)
    view = (lambda r: r[:n, :]) if axis == 0 else (lambda r: r[:, :n])
    tile = PACK16 if axis == 0 else LANES

    def tile_of(j):
        return k16_ref[j * tile:(j + 1) * tile, :] if axis == 0 else k16_ref[:, j * tile:(j + 1) * tile]

    def count(pred):
        return jnp.sum(jnp.where(pred, 1.0, 0.0), axis=axis, keepdims=True)

    def count16(cand16):
        accs = [None] * N_ACC
        for j in range(n // tile):
            one = jnp.where(tile_of(j) >= cand16, jnp.int16(1), jnp.int16(0))
            accs[j % N_ACC] = one if accs[j % N_ACC] is None else accs[j % N_ACC] + one
        tot = accs[0]
        for a in accs[1:]:
            if a is not None:
                tot = tot + a
        return jnp.sum(tot.astype(F32), axis=axis, keepdims=True)

    def search16(need):
        def step(it, thr_u):
            cand_u = thr_u | jnp.left_shift(jnp.int32(1), 15 - it)
            cnt = count16((cand_u - HALF_BIAS).astype(jnp.int16))
            return jnp.where(cnt >= need, cand_u, thr_u)

        shape = (1, key_ref.shape[1]) if axis == 0 else (key_ref.shape[0], 1)
        return lax.fori_loop(0, 16, step, jnp.zeros(shape, I32))

    key = view(key_ref)
    hi = key >> 16
    if axis == 0:
        k16_ref[:n, :] = hi.astype(jnp.int16)
    else:
        k16_ref[:, :n] = hi.astype(jnp.int16)
    thr_hi = search16(kf) - HALF_BIAS
    need_lo = kf - count(hi > thr_hi)
    lo = jnp.where(hi == thr_hi, (key & 0xFFFF) - HALF_BIAS, -HALF_BIAS).astype(jnp.int16)
    if axis == 0:
        k16_ref[:n, :] = lo
    else:
        k16_ref[:, :n] = lo
    thr = (thr_hi << 16) | search16(need_lo)

    gt = key > thr
    eq = key == thr
    need = kf - count(gt)
    tie = ((count(eq) > need) & (thr != INT_MIN)).astype(F32)

    def index_search():
        def step(it, bound):
            cand = bound | jnp.left_shift(jnp.int32(1), idx_bits - 1 - it)
            c = count((view(key_ref) == thr) & (kpos < cand))
            return jnp.where(c <= need, cand, bound)

        return lax.fori_loop(0, idx_bits, step, jnp.zeros(thr.shape, I32))

    bound = lax.cond(jnp.max(tie) > 0.5, index_search, lambda: jnp.full(thr.shape, 1 << idx_bits, I32))
    return (gt | (eq & (kpos < bound))) & (key != INT_MIN)


N_PAIRS = N_HEADS // 2


def _attn_prompt_variant(lk, qp_ref, qip_ref, w8t_ref, kb_ref, vb_ref, ki2_ref, ob_ref, sc_ref, key_ref, k16_ref,
                         bias_ref, topk, idx_bits):
    i = pl.program_id(1)
    low = lax.broadcasted_iota(I32, (QB, LANES), 1) < HEAD_DIM

    qpos = i * QB + lax.broadcasted_iota(I32, (1, QB), 1)
    kpos = lax.broadcasted_iota(I32, (lk, 1), 0)
    sc_ref[:lk, :] = jnp.zeros((lk, QB), F32)

    def idx_pair(j, carry):
        d = _dot_nt(ki2_ref[0:lk, :], qip_ref[j])
        w0 = w8t_ref[pl.ds(2 * j, 1), :]
        w1 = w8t_ref[pl.ds(2 * j + 1, 1), :]
        sc_ref[:lk, :] += w0 * jnp.maximum(d[:, :QB], 0.0) + w1 * jnp.maximum(d[:, QB:], 0.0)
        return carry

    lax.fori_loop(0, N_PAIRS, idx_pair, 0)
    key_ref[:lk, :] = _sortable_key(sc_ref[:lk, :], kpos <= qpos)
    sel = _topk_mask(key_ref, k16_ref, lk, kpos, topk, idx_bits, axis=0)
    bias_ref[:, :lk] = jnp.where(sel, 0.0, NEG_INF).T

    def att_pair(j, carry):
        s2 = _dot_nt(qp_ref[j], kb_ref[j, 0:lk, :])
        ps, ls = [], []
        for half in range(2):
            s = s2[half * QB:(half + 1) * QB] + bias_ref[:, :lk]
            m = jnp.max(s, axis=1, keepdims=True)
            p = jnp.exp(s - m)
            ls.append(jnp.sum(p, axis=1, keepdims=True))
            ps.append(p.astype(BF16))
        pv = _dot(jnp.concatenate(ps, axis=0), vb_ref[j, 0:lk, :])
        ob_ref[j] = jnp.where(low, pv[:QB] / ls[0], pv[QB:] / ls[1])
        return carry

    lax.fori_loop(0, N_PAIRS, att_pair, 0)


def _attn_prompt_body(q_ref, qi_ref, kwq_ref, k_ref, v_ref, kwk_ref, y_init_hbm, o_ref, kb_ref, vb_ref, ki2_ref,
                      qp_ref, qip_ref, w8t_ref, ob_ref, sc_ref, key_ref, k16_ref, bias_ref, *, t_pad, topk, idx_bits):
    i = pl.program_id(1)
    low = lax.broadcasted_iota(I32, (QB, LANES), 1) < HEAD_DIM

    def split_pair(tile):
        zero = jnp.zeros_like(tile)
        return jnp.concatenate([jnp.where(low, tile, zero), jnp.where(low, zero, tile)], axis=0)

    @pl.when(i == 0)
    def _():
        for j in range(N_PAIRS):
            ts = slice(j * LANES, (j + 1) * LANES)
            kb_ref[j] = k_ref[:, ts].astype(BF16)
            vb_ref[j] = v_ref[:, ts].astype(BF16)
        kw = kwk_ref[...]
        first = lax.broadcasted_iota(I32, kw.shape, 1) < IDX_DIM
        ki2_ref[...] = jnp.where(first, kw, pltpu.roll(kw, IDX_DIM, 1)).astype(BF16)

    q = (q_ref[...] * (HEAD_DIM ** -0.5)).astype(BF16)
    qi = qi_ref[...].astype(BF16)
    for j in range(N_PAIRS):
        ts = slice(j * LANES, (j + 1) * LANES)
        qp_ref[j] = split_pair(q[:, ts])
        qip_ref[j] = split_pair(qi[:, ts])
    w8t_ref[...] = kwq_ref[...].T[IDX_DIM:IDX_DIM + IDX_HEADS, :] * ((IDX_HEADS ** -0.5) * (IDX_DIM ** -0.5))

    nqb = t_pad // QB
    per = KV_STEP // QB
    lo = 0
    while lo < nqb:
        hi = min(lo + per, nqb)
        lk = hi * QB

        @pl.when((i >= lo) & (i < hi))
        def _(lk=lk):
            _attn_prompt_variant(lk, qp_ref, qip_ref, w8t_ref, kb_ref, vb_ref, ki2_ref, ob_ref, sc_ref, key_ref,
                                 k16_ref, bias_ref, topk, idx_bits)

        lo = hi
    for j in range(N_PAIRS):
        o_ref[:, j * LANES:(j + 1) * LANES] = ob_ref[j]


def attn_prompt(z, y_init, *, n_seq, t_pad, topk):
    nqb = t_pad // QB
    idx_bits = max(1, (t_pad - 1).bit_length())
    cq, ck, cv, cqi = Z_Q // ATT_W, Z_K // ATT_W, Z_V // ATT_W, Z_QI // ATT_W
    ckw = Z_KI // LANES
    body = functools.partial(_attn_prompt_body, t_pad=t_pad, topk=topk, idx_bits=idx_bits)
    return pl.pallas_call(
        body,
        grid=(n_seq, nqb),
        in_specs=[
            pl.BlockSpec((QB, ATT_W), lambda b, i: (b * nqb + i, cq)),
            pl.BlockSpec((QB, ATT_W), lambda b, i: (b * nqb + i, cqi)),
            pl.BlockSpec((QB, LANES), lambda b, i: (b * nqb + i, ckw)),
            pl.BlockSpec((t_pad, ATT_W), lambda b, i: (b, ck)),
            pl.BlockSpec((t_pad, ATT_W), lambda b, i: (b, cv)),
            pl.BlockSpec((t_pad, LANES), lambda b, i: (b, ckw)),
            pl.BlockSpec(memory_space=pl.ANY),
        ],
        out_specs=pl.BlockSpec((QB, ATT_W), lambda b, i: (b * nqb + i, 0)),
        out_shape=jax.ShapeDtypeStruct(y_init.shape, F32),
        input_output_aliases={6: 0},
        scratch_shapes=[pltpu.VMEM((N_PAIRS, t_pad, LANES), BF16), pltpu.VMEM((N_PAIRS, t_pad, LANES), BF16),
                        pltpu.VMEM((t_pad, LANES), BF16),
                        pltpu.VMEM((N_PAIRS, 2 * QB, LANES), BF16), pltpu.VMEM((N_PAIRS, 2 * QB, LANES), BF16),
                        pltpu.VMEM((IDX_HEADS, QB), F32), pltpu.VMEM((N_PAIRS, QB, LANES), F32),
                        pltpu.VMEM((t_pad, QB), F32),
                        pltpu.VMEM((t_pad, QB), I32), pltpu.VMEM((t_pad, QB), jnp.int16),
                        pltpu.VMEM((QB, t_pad), F32)],
        compiler_params=_params(("parallel", "arbitrary")),
        name="attn_prompt",
    )(z, z, z, z, z, z, y_init)


IDX_PAGES = 128
KV_PAGES = 8
KV_SLOTS = 3


SEL_SEQS = 8


def _sample_select_body(pt_ref, qi_ref, kwn_ref, ckidx_hbm, mask_ref, kibuf, sc_ref, key_ref, k16_ref, sem_ki, *,
                        layer, n_pages, ds, topk, idx_bits):
    g = pl.program_id(0)
    past = n_pages * PAGE
    ltot = past + LANES
    idx_pages = kibuf.shape[1]
    n_ch = n_pages // idx_pages
    rows = sc_ref.shape[0]
    sel_seqs = rows // ds
    n_steps = sel_seqs * n_ch
    width = idx_pages * PAGE

    def ki_copy(step, slot, p):
        seq = g * sel_seqs + step // n_ch
        page = pt_ref[seq, (step % n_ch) * idx_pages + p]
        return pltpu.make_async_copy(ckidx_hbm.at[layer, page], kibuf.at[slot, p], sem_ki.at[slot])

    def start_ki(step, slot):
        for p in range(idx_pages):
            ki_copy(step, slot, p).start()

    def wait_ki(step, slot):
        for p in range(idx_pages):
            ki_copy(step, slot, p).wait()

    start_ki(0, 0)

    def seq_operands(s):
        r0 = s * ds if isinstance(s, int) else pl.multiple_of(s * ds, ds)
        qi = qi_ref[pl.ds(r0, ds), :]
        kw = kwn_ref[pl.ds(r0, ds), :]
        qh = jnp.concatenate([qi[:, h * IDX_DIM:(h + 1) * IDX_DIM] for h in range(IDX_HEADS)], axis=0)
        w8 = kw[:, IDX_DIM:IDX_DIM + IDX_HEADS] * (IDX_HEADS ** -0.5)
        wcol = jnp.concatenate([w8[:, h:h + 1] for h in range(IDX_HEADS)], axis=0)
        return r0, qh.astype(BF16), wcol, kw[:, 0:IDX_DIM]

    def idx_scores(wcol, d):
        r = wcol * jnp.maximum(d * (IDX_DIM ** -0.5), 0.0)
        acc = r[0:ds]
        for h in range(1, IDX_HEADS):
            acc = acc + r[h * ds:(h + 1) * ds]
        return acc

    def idx_step(step, carry):
        slot = step % 2

        @pl.when(step + 1 < n_steps)
        def _():
            start_ki(step + 1, 1 - slot)

        wait_ki(step, slot)
        r0, qh, wcol, _ = seq_operands(step // n_ch)
        kit = jnp.concatenate([kibuf[slot, p] for p in range(idx_pages)], axis=1).astype(BF16)
        cols = pl.ds(pl.multiple_of((step % n_ch) * width, width), width)
        sc_ref[pl.ds(r0, ds), cols] = idx_scores(wcol, _dot(qh, kit))
        return carry

    lax.fori_loop(0, n_steps, idx_step, 0)
    for s in range(sel_seqs):
        r0, qh, wcol, ki_new = seq_operands(s)
        ki_pad = jnp.concatenate([ki_new, jnp.zeros((LANES - ds, IDX_DIM), F32)], axis=0).astype(BF16)
        sc_ref[r0:r0 + ds, past:ltot] = idx_scores(wcol, _dot_nt(qh, ki_pad))

    qpos = past + lax.broadcasted_iota(I32, (rows, 1), 0) % ds
    kpos = lax.broadcasted_iota(I32, (1, ltot), 1)
    key_ref[...] = _sortable_key(sc_ref[...], kpos <= qpos)
    sel = _topk_mask(key_ref, k16_ref, ltot, kpos, topk, idx_bits, axis=1)
    mask_ref[...] = jnp.where(sel, 1.0, 0.0)


def _sample_attend_body(pt_ref, q_ref, kn_ref, vn_ref, mask_ref, ck_hbm, cv_hbm, ya_hbm, o_ref, kbuf, vbuf, sem_k, sem_v,
                        *, layer, n_pages, ds):
    b = pl.program_id(0)
    past = n_pages * PAGE
    ltot = past + LANES
    n_ch = n_pages // KV_PAGES
    rows_h = N_HEADS * ds

    n_total = pl.num_programs(0) * n_ch

    def kv_copy(g, p):
        slot = g % KV_SLOTS
        page = pt_ref[g // n_ch, (g % n_ch) * KV_PAGES + p]
        return (pltpu.make_async_copy(ck_hbm.at[layer, page], kbuf.at[slot, p], sem_k.at[slot]),
                pltpu.make_async_copy(cv_hbm.at[layer, page], vbuf.at[slot, p], sem_v.at[slot]))

    def start_kv(g):
        for p in range(KV_PAGES):
            ck, cv = kv_copy(g, p)
            ck.start()
            cv.start()

    def wait_kv(g):
        for p in range(KV_PAGES):
            ck, cv = kv_copy(g, p)
            ck.wait()
            cv.wait()

    @pl.when(b == 0)
    def _():
        for g in range(KV_SLOTS - 1):
            start_kv(g)

    q8 = jnp.concatenate([q_ref[...] * (HEAD_DIM ** -0.5)] * N_HEADS, axis=0)
    rhead = lax.broadcasted_iota(I32, (rows_h, ATT_W), 0) // ds
    chead = lax.broadcasted_iota(I32, (rows_h, ATT_W), 1) // HEAD_DIM
    qbd = jnp.where(rhead == chead, q8, 0.0).astype(BF16)

    def attend(s, msk, pv_fn, state):
        m_old, l_old, acc = state
        on = jnp.concatenate([msk] * N_HEADS, axis=0) > 0.5
        s = jnp.where(on, s, NEG_INF)
        m_new = jnp.maximum(m_old, jnp.max(s, axis=1, keepdims=True))
        alpha = jnp.exp(m_old - m_new)
        p = jnp.where(on, jnp.exp(s - m_new), 0.0)
        l_new = alpha * l_old + jnp.sum(p, axis=1, keepdims=True)
        return m_new, l_new, alpha * acc + pv_fn(p.astype(BF16))

    def kv_step(c, state):
        g = b * n_ch + c
        slot = g % KV_SLOTS
        ahead = g + (KV_SLOTS - 1)

        @pl.when(ahead < n_total)
        def _():
            start_kv(ahead)

        wait_kv(g)
        width = KV_PAGES * PAGE
        s = jnp.concatenate(
            [_dot(qbd, kbuf[slot, p].reshape(ATT_W, PAGE).astype(BF16)) for p in range(KV_PAGES)], axis=1)

        def pv_fn(p):
            acc = _dot_nt(p[:, 0:PAGE], vbuf[slot, 0].reshape(ATT_W, PAGE).astype(BF16))
            for j in range(1, KV_PAGES):
                acc = acc + _dot_nt(p[:, j * PAGE:(j + 1) * PAGE],
                                    vbuf[slot, j].reshape(ATT_W, PAGE).astype(BF16))
            return acc

        msk = mask_ref[:, pl.ds(pl.multiple_of(c * width, width), width)]
        return attend(s, msk, pv_fn, state)

    state = (jnp.full((rows_h, 1), NEG_INF, F32), jnp.zeros((rows_h, 1), F32),
             jnp.zeros((rows_h, ATT_W), F32))
    state = lax.fori_loop(0, n_ch, kv_step, state)
    pad = jnp.zeros((LANES - ds, ATT_W), F32)
    kn = jnp.concatenate([kn_ref[...], pad], axis=0).astype(BF16)
    vn = jnp.concatenate([vn_ref[...], pad], axis=0).astype(BF16)
    _, l_fin, acc = attend(_dot_nt(qbd, kn), mask_ref[:, past:ltot], lambda p: _dot(p, vn), state)
    out = acc / l_fin
    for h in range(N_HEADS):
        hs = slice(h * HEAD_DIM, (h + 1) * HEAD_DIM)
        o_ref[:, hs] = out[h * ds:(h + 1) * ds, hs]


def attn_sample(z, page_table, cache_kidx, cache_k, cache_v, ya, *, layer, row0, n_seq, ds, topk):
    n_pages = page_table.shape[1]
    past = n_pages * PAGE
    ltot = past + LANES
    idx_bits = max(1, (ltot - 1).bit_length())
    cq, ck, cv, cqi = Z_Q // ATT_W, Z_K // ATT_W, Z_V // ATT_W, Z_QI // ATT_W
    ckw = Z_KI // LANES
    sel_seqs = min(SEL_SEQS, n_seq)
    rows = sel_seqs * ds
    rbs = row0 // rows
    assert n_seq % sel_seqs == 0 and row0 % rows == 0
    sel_body = functools.partial(_sample_select_body, layer=layer, n_pages=n_pages, ds=ds, topk=topk,
                                 idx_bits=idx_bits)
    mask = pl.pallas_call(
        sel_body,
        grid_spec=pltpu.PrefetchScalarGridSpec(
            num_scalar_prefetch=1,
            grid=(n_seq // sel_seqs,),
            in_specs=[
                pl.BlockSpec((rows, ATT_W), lambda g, pt: (rbs + g, cqi)),
                pl.BlockSpec((rows, LANES), lambda g, pt: (rbs + g, ckw)),
                pl.BlockSpec(memory_space=pl.ANY),
            ],
            out_specs=pl.BlockSpec((rows, ltot), lambda g, pt: (g, 0)),
            scratch_shapes=[
                pltpu.VMEM((2, min(IDX_PAGES, n_pages), IDX_DIM, PAGE), F32),
                pltpu.VMEM((rows, ltot), F32),
                pltpu.VMEM((rows, ltot), I32),
                pltpu.VMEM((rows, ltot), jnp.int16),
                pltpu.SemaphoreType.DMA((2,)),
            ],
        ),
        out_shape=jax.ShapeDtypeStruct((n_seq * ds, ltot), F32),
        compiler_params=_params(("arbitrary",)),
        name="sample_select",
    )(page_table, z, z, cache_kidx)

    rb = row0 // ds
    att_body = functools.partial(_sample_attend_body, layer=layer, n_pages=n_pages, ds=ds)
    return pl.pallas_call(
        att_body,
        grid_spec=pltpu.PrefetchScalarGridSpec(
            num_scalar_prefetch=1,
            grid=(n_seq,),
            in_specs=[
                pl.BlockSpec((ds, ATT_W), lambda b, pt: (rb + b, cq)),
                pl.BlockSpec((ds, ATT_W), lambda b, pt: (rb + b, ck)),
                pl.BlockSpec((ds, ATT_W), lambda b, pt: (rb + b, cv)),
                pl.BlockSpec((ds, ltot), lambda b, pt: (b, 0)),
                pl.BlockSpec(memory_space=pl.ANY),
                pl.BlockSpec(memory_space=pl.ANY),
                pl.BlockSpec(memory_space=pl.ANY),
            ],
            out_specs=pl.BlockSpec((ds, ATT_W), lambda b, pt: (rb + b, 0)),
            scratch_shapes=[
                pltpu.VMEM((KV_SLOTS, KV_PAGES, N_HEADS, HEAD_DIM, PAGE), F32),
                pltpu.VMEM((KV_SLOTS, KV_PAGES, N_HEADS, HEAD_DIM, PAGE), F32),
                pltpu.SemaphoreType.DMA((KV_SLOTS,)),
                pltpu.SemaphoreType.DMA((KV_SLOTS,)),
            ],
        ),
        out_shape=jax.ShapeDtypeStruct(ya.shape, F32),
        input_output_aliases={7: 0},
        compiler_params=_params(("arbitrary",)),
        name="sample_attend",
    )(page_table, z, z, z, mask, cache_k, cache_v, ya)


def _hgrn_chunk(zq, zf, vv, zg, valid, lbf, gn, st_ref):
    c = zq.shape[0]
    levels = c.bit_length() - 1
    e = jnp.exp(-jnp.abs(zf))
    r = 1.0 / (1.0 + e)
    er = e * r
    sig = jnp.where(zf >= 0.0, r, er)
    f = lbf + (1.0 - lbf) * sig
    logf = jnp.where(valid, jnp.log(f), 0.0)
    kk = jnp.where(valid, (1.0 - lbf) * jnp.where(zf >= 0.0, er, r), 0.0)
    qq = zq * _sigmoid(zq)
    vb = jnp.where(valid, vv, 0.0).astype(BF16)

    tq = lax.broadcasted_iota(I32, (c, 1), 0)
    ts = lax.broadcasted_iota(I32, (1, c), 1)
    tril = (ts <= tq).astype(BF16)
    hi = logf.astype(BF16)
    r1 = logf - hi.astype(F32)
    mid = r1.astype(BF16)
    lo = (r1 - mid.astype(F32)).astype(BF16)
    bcum = _dot(tril, hi) + _dot(tril, mid) + _dot(tril, lo)

    x = tq ^ ts
    heads = [slice(h * HG_DK, (h + 1) * HG_DK) for h in range(HG_HEADS)]
    qb0, kb0 = qq.astype(BF16), kk.astype(BF16)
    amat = [jnp.where(x == 0, _dot_nt(qb0[:, hs], kb0[:, hs]), 0.0) for hs in heads]
    gprev = jnp.where(tq == 0, 0.0, pltpu.roll(bcum, 1, 0))
    hend = bcum
    for lev in range(levels):
        m = 1 << lev
        qs = (qq * jnp.exp(bcum - gprev)).astype(BF16)
        ks = (kk * jnp.exp(hend - bcum)).astype(BF16)
        sib = ((x >> lev) == 1) & (ts < tq)
        for h, hs in enumerate(heads):
            amat[h] = amat[h] + jnp.where(sib, _dot_nt(qs[:, hs], ks[:, hs]), 0.0)
        upper = ((tq >> lev) & 1) == 1
        gprev = jnp.where(upper, pltpu.roll(gprev, m, 0), gprev)
        hend = jnp.where(upper, hend, pltpu.roll(hend, c - m, 0))
    qs = (qq * jnp.exp(bcum)).astype(BF16)
    ks = (kk * jnp.exp(hend - bcum)).astype(BF16)
    decay = jnp.exp(hend[0:1, :])
    outs = []
    for h, hs in enumerate(heads):
        st = st_ref[h]
        o = _dot_nt(qs[:, hs], st.astype(BF16)) + _dot(amat[h].astype(BF16), vb[:, hs])
        st_ref[h] = st * decay[:, hs] + _dot_tn(vb[:, hs], ks[:, hs])
        ms = jnp.mean(o * o, axis=-1, keepdims=True)
        g = zg[:, hs]
        outs.append(o * lax.rsqrt(ms + RMS_EPS) * gn * (g * _sigmoid(g)))
    return jnp.concatenate(outs, axis=1)


def _lower_bound(lbraw_ref, layer):
    r = lbraw_ref[...]
    e = jnp.exp(r - jnp.max(r, axis=0, keepdims=True))
    sm = e / jnp.sum(e, axis=0, keepdims=True)
    cs = sm[0:1]
    for l in range(1, layer + 1):
        cs = cs + sm[l:l + 1]
    return jnp.maximum(cs - sm[0:1], LB_FLOOR)


def _hgrn_body(hq_ref, hf_ref, hv_ref, hg_ref, lbraw_ref, gn_ref, s0_ref, y_ref, sout_ref, st_ref, *,
               layer, rows_in, n_valid):
    c = HG_CHUNK
    lbf = _lower_bound(lbraw_ref, layer)
    gn = gn_ref[...]
    for h in range(HG_HEADS):
        st_ref[h] = s0_ref[0, h].T
    tq = lax.broadcasted_iota(I32, (c, 1), 0)
    if rows_in < c:
        pad = jnp.zeros((c - rows_in, HG_W), F32)

        def padded(ref):
            return jnp.concatenate([ref[...], pad], axis=0)

        y = _hgrn_chunk(padded(hq_ref), padded(hf_ref), padded(hv_ref), padded(hg_ref), tq < n_valid, lbf,
                        gn, st_ref)
        y_ref[...] = y[0:rows_in]
    else:
        def step(ci, carry):
            rows = pl.ds(pl.multiple_of(ci * c, c), c)
            y_ref[rows, :] = _hgrn_chunk(hq_ref[rows, :], hf_ref[rows, :], hv_ref[rows, :], hg_ref[rows, :],
                                         (ci * c + tq) < n_valid, lbf, gn, st_ref)
            return carry

        lax.fori_loop(0, rows_in // c, step, 0)
    for h in range(HG_HEADS):
        sout_ref[0, h] = st_ref[h].T


def hgrn(z, lb_raw, hg_norm, s0, y_prev, *, layer, row0, n_seq, rows_in, n_valid, out_rows):
    rb = row0 // rows_in
    depth = lb_raw.shape[0]
    body = functools.partial(_hgrn_body, layer=layer, rows_in=rows_in, n_valid=n_valid)
    cols = [Z_HQ // HG_W, Z_HF // HG_W, Z_HV // HG_W, Z_HG // HG_W]
    zspec = [pl.BlockSpec((rows_in, HG_W), functools.partial(lambda b, cc: (rb + b, cc), cc=cc)) for cc in cols]
    operands = [z, z, z, z, lb_raw, hg_norm, s0]
    in_specs = zspec + [
        pl.BlockSpec((depth, HG_W), lambda b: (0, 0)),
        pl.BlockSpec((1, HG_DV), lambda b: (0, 0)),
        pl.BlockSpec((1, HG_HEADS, HG_DK, HG_DV), lambda b: (b, 0, 0, 0)),
    ]
    aliases = {}
    if y_prev is not None:
        aliases = {len(operands): 0}
        operands.append(y_prev)
        in_specs.append(pl.BlockSpec(memory_space=pl.ANY))
        body = functools.partial(_hgrn_body_aliased, body)
    return pl.pallas_call(
        body,
        grid=(n_seq,),
        in_specs=in_specs,
        out_specs=[
            pl.BlockSpec((rows_in, HG_W), lambda b: (rb + b, 0)),
            pl.BlockSpec((1, HG_HEADS, HG_DK, HG_DV), lambda b: (b, 0, 0, 0)),
        ],
        out_shape=[
            jax.ShapeDtypeStruct((out_rows, HG_W), F32),
            jax.ShapeDtypeStruct((n_seq, HG_HEADS, HG_DK, HG_DV), F32),
        ],
        scratch_shapes=[pltpu.VMEM((HG_HEADS, HG_DV, HG_DK), F32)],
        input_output_aliases=aliases,
        compiler_params=_params(("parallel",)),
        name="hgrn",
    )(*operands)


def _hgrn_body_aliased(body, hq_ref, hf_ref, hv_ref, hg_ref, lbraw_ref, gn_ref, s0_ref, y_prev_hbm, *rest):
    del y_prev_hbm
    body(hq_ref, hf_ref, hv_ref, hg_ref, lbraw_ref, gn_ref, s0_ref, *rest)


def _permute_w_in(w):
    o = ORIG_OFFS
    parts = [w[:, o["q"]:o["q"] + 2048],
             w[:, o["hq"]:o["hq"] + 2048],
             w[:, o["ga"]:o["ga"] + 2048],
             w[:, o["ki"]:o["ki"] + IDX_DIM],
             w[:, o["wi"]:o["wi"] + IDX_HEADS],
             jnp.zeros((w.shape[0], Z_W - Z_WI - IDX_HEADS), w.dtype)]
    return jnp.concatenate(parts, axis=1).astype(BF16)


ROW_TILE = 768
FF_TILE = D_FF // 2


def kernel(x_prompt, x_sample, cache_k, cache_v, cache_kidx, state_hgrn, page_table, meta_tokens, norm_gains,
           ffn1_w_gu, ffn1_w_down, w_in, lb_raw, hg_norm, w_branch_attn, w_branch_hgrn, w_out, ffn2_w_gu,
           ffn2_w_down):
    return _forward(x_prompt, x_sample, cache_k, cache_v, cache_kidx, state_hgrn, page_table, meta_tokens,
                    norm_gains, ffn1_w_gu, ffn1_w_down, w_in, lb_raw, hg_norm, w_branch_attn, w_branch_hgrn,
                    w_out, ffn2_w_gu, ffn2_w_down, tm=ROW_TILE)


def _forward(x_prompt, x_sample, cache_k, cache_v, cache_kidx, state_hgrn, page_table, meta_tokens, norm_gains,
             ffn1_w_gu, ffn1_w_down, w_in, lb_raw, hg_norm, w_branch_attn, w_branch_hgrn, w_out, ffn2_w_gu,
             ffn2_w_down, *, tm):
    n_seq, seq, d = x_prompt.shape
    db, ds, _ = x_sample.shape
    depth = w_in.shape[0]
    t_real = seq + N_META
    t_pad = -(-t_real // QB) * QB
    rows_p = n_seq * t_pad
    rows = rows_p + db * ds
    assert rows % tm == 0
    topk_p = min(TOPK, t_real // 4)
    topk_s = min(TOPK, (page_table.shape[1] * PAGE + ds) // 4)

    meta = jnp.broadcast_to(meta_tokens[None].astype(F32), (n_seq, N_META, d))
    xp = jnp.concatenate([meta, x_prompt, jnp.zeros((n_seq, t_pad - t_real, d), F32)], axis=1)
    x = jnp.concatenate([xp.reshape(rows_p, d), x_sample.reshape(db * ds, d)], axis=0)

    ck = cache_k.transpose(0, 1, 3, 4, 2)
    cv = cache_v.transpose(0, 1, 3, 4, 2)
    cki = cache_kidx.transpose(0, 1, 3, 2)
    zero_state = jnp.zeros((n_seq, HG_HEADS, HG_DK, HG_DV), F32)
    y_zero = jnp.zeros((rows, ATT_W), F32)
    hgn = hg_norm.reshape(depth, 1, HG_DV)

    outs = {k: [] for k in ("kp", "vp", "kip", "sp", "ks", "vs", "kis", "ss")}
    for l in range(depth):
        g = norm_gains[l].reshape(6, 1, d)
        x = ffn(x, g[0], g[1], ffn1_w_gu[l].astype(BF16), ffn1_w_down[l].astype(BF16), tm=tm, tf=FF_TILE)
        z = inproj(x, g[2], _permute_w_in(w_in[l]), tm=tm, tn=896)
        ya = attn_prompt(z, y_zero, n_seq=n_seq, t_pad=t_pad, topk=topk_p)
        ya = attn_sample(z, page_table, cki, ck, cv, ya, layer=l, row0=rows_p, n_seq=db, ds=ds, topk=topk_s)
        yh, st_p = hgrn(z, lb_raw, hgn[l], zero_state, y_zero, layer=l, row0=0, n_seq=n_seq, rows_in=t_pad,
                        n_valid=t_real, out_rows=rows)
        yh, st_s = hgrn(z, lb_raw, hgn[l], state_hgrn[l], yh, layer=l, row0=rows_p, n_seq=db, rows_in=ds,
                        n_valid=ds, out_rows=rows)
        x = combine(x, ya, yh, z, w_branch_attn[l].astype(BF16), w_branch_hgrn[l].astype(BF16),
                    w_out[l].astype(BF16), g[3], tm=tm)
        x = ffn(x, g[4], g[5], ffn2_w_gu[l].astype(BF16), ffn2_w_down[l].astype(BF16), tm=tm, tf=FF_TILE)

        def cols_p(c0, w):
            return lax.slice(z, (0, c0), (rows_p, c0 + w)).reshape(n_seq, t_pad, w)[:, :t_real]

        def cols_s(c0, w):
            return lax.slice(z, (rows_p, c0), (rows, c0 + w)).reshape(db, ds, w)

        outs["kp"].append(cols_p(Z_K, ATT_W).reshape(n_seq, t_real, N_HEADS, HEAD_DIM))
        outs["vp"].append(cols_p(Z_V, ATT_W).reshape(n_seq, t_real, N_HEADS, HEAD_DIM))
        outs["kip"].append(cols_p(Z_KI, IDX_DIM))
        outs["sp"].append(st_p)
        outs["ks"].append(cols_s(Z_K, ATT_W).reshape(db, ds, N_HEADS, HEAD_DIM))
        outs["vs"].append(cols_s(Z_V, ATT_W).reshape(db, ds, N_HEADS, HEAD_DIM))
        outs["kis"].append(cols_s(Z_KI, IDX_DIM))
        outs["ss"].append(st_s)

    y_prompt = x[:rows_p].reshape(n_seq, t_pad, d)[:, N_META:t_real]
    y_sample = x[rows_p:].reshape(db, ds, d)
    st = lambda k: jnp.stack(outs[k])
    return (y_prompt, y_sample, st("kp"), st("vp"), st("kip"), st("sp"), st("ks"), st("vs"), st("kis"),
            st("ss"))
```
